```python
import math
import jax, jax.numpy as jnp
from jax import lax
import numpy as np

D_MODEL = 2048
BATCH = 4
SEQ = 4096
DEPTH = 1

RET_HEADS = 8
RET_QK_DIM = 128
RET_V_DIM = 256
RET_QK_WIDTH = RET_HEADS * RET_QK_DIM
RET_V_WIDTH = RET_HEADS * RET_V_DIM
RET_CHUNK = 128
ROPE_BASE = 10000.0
S5_GROUP = 16
S5_WIDTH = D_MODEL // 2
S5_GROUPS = S5_WIDTH // S5_GROUP
S5_STATE = 64
DT_MIN = 1e-3
DT_MAX = 1e-1
D_FF = -(-8 * D_MODEL // (3 * 256)) * 256
NORM_EPS = 1e-6
GN_EPS = 1e-5

IN_SIZES = (RET_QK_WIDTH, RET_QK_WIDTH, RET_V_WIDTH, RET_V_WIDTH, S5_WIDTH, D_MODEL, D_MODEL)
IN_WIDTH = sum(IN_SIZES)
IN_SPLITS = tuple(int(s) for s in np.cumsum(IN_SIZES)[:-1])

kernel_name = "hybrid_retention_s5_gated_block"


def rmsnorm(x, g):
    xf = x.astype(jnp.float32)
    y = xf * lax.rsqrt(jnp.mean(xf * xf, axis=-1, keepdims=True) + NORM_EPS)
    return (y * g.astype(jnp.float32)).astype(x.dtype)


def head_group_norm(y):
    yf = y.astype(jnp.float32)
    mu = jnp.mean(yf, axis=-1, keepdims=True)
    var = jnp.mean(jnp.square(yf - mu), axis=-1, keepdims=True)
    return ((yf - mu) * lax.rsqrt(var + GN_EPS)).astype(y.dtype)


def rope(t, cos, sin):
    t1, t2 = jnp.split(t, 2, axis=-1)
    return jnp.concatenate([t1 * cos - t2 * sin, t1 * sin + t2 * cos], axis=-1)


def retention(q, k, v):
    Bn, L, H, dk = q.shape
    dv = v.shape[-1]
    C = RET_CHUNK
    N = L // C
    dt = q.dtype
    log_g = jnp.log1p(-jnp.exp2(-5.0 - jnp.arange(H, dtype=jnp.float32)))
    idx = jnp.arange(C, dtype=jnp.float32)
    rel = idx[:, None] - idx[None, :]
    decay = jnp.where(rel[None] >= 0,
                      jnp.exp(jnp.maximum(rel, 0.0)[None] * log_g[:, None, None]), 0.0)
    w_state = jnp.exp((C - 1.0 - idx)[:, None] * log_g[None, :])
    w_cross = jnp.exp((idx + 1.0)[:, None] * log_g[None, :])
    chunk_decay = jnp.exp(C * log_g)

    qc = q.reshape(Bn, N, C, H, dk)
    kc = k.reshape(Bn, N, C, H, dk)
    vc = v.reshape(Bn, N, C, H, dv)

    scores = jnp.einsum('bnchd,bnshd->bnhcs', qc, kc) * decay.astype(dt)[None, None]
    inner = jnp.einsum('bnhcs,bnshv->bnchv', scores, vc)

    kv = jnp.einsum('bnchd,ch,bnchv->bnhdv', kc, w_state.astype(dt), vc)
    cd = chunk_decay.astype(dt)[None, :, None, None]

    def step(R, kv_n):
        return cd * R + kv_n, R

    R0 = jnp.zeros((Bn, H, dk, dv), dtype=kv.dtype)
    _, R_prev = lax.scan(step, R0, jnp.moveaxis(kv, 1, 0))
    R_prev = jnp.moveaxis(R_prev, 0, 1)

    cross = jnp.einsum('bnchd,bnhdv->bnchv', qc, R_prev) * w_cross.astype(dt)[None, None, :, :, None]
    return (inner + cross).reshape(Bn, L, H, dv)


def s5_ssm(u, a_re, a_im, log_dt, b_re, b_im, c_re, c_im, d_skip):
    Bn, L, _ = u.shape
    ug = u.reshape(Bn, L, S5_GROUPS, S5_GROUP)
    dt = jnp.exp(log_dt)[:, None]
    mag = jnp.exp(a_re * dt)
    lb_re = mag * jnp.cos(a_im * dt)
    lb_im = mag * jnp.sin(a_im * dt)
    nr = lb_re - 1.0
    den = a_re * a_re + a_im * a_im
    f_re = (nr * a_re + lb_im * a_im) / den
    f_im = (lb_im * a_re - nr * a_im) / den
    bb_re = f_re[..., None] * b_re - f_im[..., None] * b_im
    bb_im = f_re[..., None] * b_im + f_im[..., None] * b_re
    bu_re = jnp.einsum('blgh,gph->blgp', ug, bb_re)
    bu_im = jnp.einsum('blgh,gph->blgp', ug, bb_im)
    A_re = jnp.broadcast_to(lb_re, (L,) + lb_re.shape)
    A_im = jnp.broadcast_to(lb_im, (L,) + lb_im.shape)

    def combine(e1, e2):
        a1r, a1i, b1r, b1i = e1
        a2r, a2i, b2r, b2i = e2
        return (a1r * a2r - a1i * a2i,
                a1r * a2i + a1i * a2r,
                a2r * b1r - a2i * b1i + b2r,
                a2r * b1i + a2i * b1r + b2i)

    def scan_one(br, bi):
        _, _, xr, xi = lax.associative_scan(combine, (A_re, A_im, br, bi), axis=0)
        return xr, xi

    x_re, x_im = jax.vmap(scan_one)(bu_re, bu_im)
    y = (jnp.einsum('blgp,ghp->blgh', x_re, c_re)
         - jnp.einsum('blgp,ghp->blgh', x_im, c_im)
         + d_skip[None, None] * ug)
    return y.reshape(Bn, L, S5_WIDTH)


def token_mixer(h, w_in, w_ret_out, a_re, a_im, log_dt, b_re, b_im, c_re, c_im, d_skip,
                w_s5_glu, w_out, cos, sin):
    Bn, L, _ = h.shape
    proj = h @ w_in
    q, k, v, g_ret, u, gate_r, gate_s = jnp.split(proj, IN_SPLITS, axis=-1)
    q = rope(q.reshape(Bn, L, RET_HEADS, RET_QK_DIM), cos, sin)
    k = rope(k.reshape(Bn, L, RET_HEADS, RET_QK_DIM), cos, sin) * (RET_QK_DIM ** -0.5)
    v = v.reshape(Bn, L, RET_HEADS, RET_V_DIM)
    ret = head_group_norm(retention(q, k, v)).reshape(Bn, L, RET_V_WIDTH)
    y_ret = (jax.nn.silu(g_ret) * ret) @ w_ret_out
    y_ssm = jax.nn.gelu(s5_ssm(u, a_re, a_im, log_dt, b_re, b_im, c_re, c_im, d_skip))
    glu_a, glu_b = jnp.split(y_ssm @ w_s5_glu, 2, axis=-1)
    y_s5 = glu_a * jax.nn.sigmoid(glu_b)
    merged = jax.nn.sigmoid(gate_r) * y_ret + jax.nn.sigmoid(gate_s) * y_s5
    return merged @ w_out


def swiglu_ffn(h, w_ffn_in, w_ffn_out):
    a, b = jnp.split(h @ w_ffn_in, 2, axis=-1)
    return (jax.nn.silu(a) * b) @ w_ffn_out


def setup_inputs(seed: int = 0) -> dict:
    key = jax.random.key(seed)
    ks = jax.random.split(key, 20)
    f32 = jnp.float32

    def dense(k, shape, fan_in):
        return jax.random.normal(k, shape, f32) * (fan_in ** -0.5)

    x = jax.random.normal(ks[0], (BATCH, SEQ, D_MODEL), f32)
    c = jax.random.normal(ks[1], (BATCH, D_MODEL), f32)
    w_ada = dense(ks[2], (DEPTH, D_MODEL, 6 * D_MODEL), D_MODEL)
    b_ada = 0.01 * jax.random.normal(ks[3], (DEPTH, 6 * D_MODEL), f32)
    norm_gains = 1.0 + 0.05 * jax.random.normal(ks[4], (DEPTH, 4, D_MODEL), f32)
    w_in = dense(ks[5], (DEPTH, D_MODEL, IN_WIDTH), D_MODEL)
    w_ret_out = dense(ks[6], (DEPTH, RET_V_WIDTH, D_MODEL), RET_V_WIDTH)
    ssm_a_re = -0.5 + 0.01 * jax.random.normal(ks[7], (DEPTH, S5_GROUPS, S5_STATE), f32)
    ssm_a_im = jnp.broadcast_to(math.pi * jnp.arange(S5_STATE, dtype=f32),
                                (DEPTH, S5_GROUPS, S5_STATE))
    ssm_log_dt = jax.random.uniform(ks[8], (DEPTH, S5_GROUPS), f32,
                                    math.log(DT_MIN), math.log(DT_MAX))
    ssm_b_re = dense(ks[9], (DEPTH, S5_GROUPS, S5_STATE, S5_GROUP), 2 * S5_GROUP)
    ssm_b_im = dense(ks[10], (DEPTH, S5_GROUPS, S5_STATE, S5_GROUP), 2 * S5_GROUP)
    ssm_c_re = 0.5 * jax.random.normal(ks[11], (DEPTH, S5_GROUPS, S5_GROUP, S5_STATE), f32)
    ssm_c_im = 0.5 * jax.random.normal(ks[12], (DEPTH, S5_GROUPS, S5_GROUP, S5_STATE), f32)
    ssm_d = jax.random.normal(ks[13], (DEPTH, S5_GROUPS, S5_GROUP), f32)
    w_s5_glu = dense(ks[14], (DEPTH, S5_WIDTH, 2 * D_MODEL), S5_WIDTH)
    w_out = dense(ks[15], (DEPTH, D_MODEL, D_MODEL), D_MODEL)
    w_ffn_in = dense(ks[16], (DEPTH, D_MODEL, 2 * D_FF), D_MODEL)
    w_ffn_out = dense(ks[17], (DEPTH, D_FF, D_MODEL), D_FF)
    return {"x": x, "c": c, "w_ada": w_ada, "b_ada": b_ada, "norm_gains": norm_gains,
            "w_in": w_in, "w_ret_out": w_ret_out, "ssm_a_re": ssm_a_re, "ssm_a_im": ssm_a_im,
            "ssm_log_dt": ssm_log_dt, "ssm_b_re": ssm_b_re, "ssm_b_im": ssm_b_im,
            "ssm_c_re": ssm_c_re, "ssm_c_im": ssm_c_im, "ssm_d": ssm_d, "w_s5_glu": w_s5_glu,
            "w_out": w_out, "w_ffn_in": w_ffn_in, "w_ffn_out": w_ffn_out}


def reference(x, c, w_ada, b_ada, norm_gains, w_in, w_ret_out, ssm_a_re, ssm_a_im, ssm_log_dt,
              ssm_b_re, ssm_b_im, ssm_c_re, ssm_c_im, ssm_d, w_s5_glu, w_out, w_ffn_in, w_ffn_out):
    L = x.shape[1]
    pos = jnp.arange(L, dtype=jnp.float32)
    inv_freq = ROPE_BASE ** (-jnp.arange(RET_QK_DIM // 2, dtype=jnp.float32) * (2.0 / RET_QK_DIM))
    ang = pos[:, None] * inv_freq[None, :]
    cos = jnp.cos(ang)[None, :, None, :].astype(x.dtype)
    sin = jnp.sin(ang)[None, :, None, :].astype(x.dtype)
    c_act = jax.nn.silu(c)
    for l in range(DEPTH):
        mod = c_act @ w_ada[l] + b_ada[l]
        sh_m, sc_m, gt_m, sh_f, sc_f, gt_f = [m[:, None, :] for m in jnp.split(mod, 6, axis=-1)]
        g = norm_gains[l]
        h = rmsnorm(x, g[0]) * (1.0 + sc_m) + sh_m
        y = token_mixer(h, w_in[l], w_ret_out[l], ssm_a_re[l], ssm_a_im[l], ssm_log_dt[l],
                        ssm_b_re[l], ssm_b_im[l], ssm_c_re[l], ssm_c_im[l], ssm_d[l],
                        w_s5_glu[l], w_out[l], cos, sin)
        x = x + gt_m * rmsnorm(y, g[1])
        h = rmsnorm(x, g[2]) * (1.0 + sc_f) + sh_f
        y = swiglu_ffn(h, w_ffn_in[l], w_ffn_out[l])
        x = x + gt_f * rmsnorm(y, g[3])
    return x
```

```python
import functools
import math

import jax
import jax.numpy as jnp
from jax import lax
from jax.experimental import pallas as pl
from jax.experimental.pallas import tpu as pltpu

F32 = jnp.float32
BF16 = jnp.bfloat16

D_MODEL = 2048
RET_HEADS = 8
RET_QK_DIM = 128
RET_V_DIM = 256
RET_QK_WIDTH = RET_HEADS * RET_QK_DIM
RET_V_WIDTH = RET_HEADS * RET_V_DIM
ROPE_BASE = 10000.0
S5_GROUP = 16
S5_WIDTH = D_MODEL // 2
S5_GROUPS = S5_WIDTH // S5_GROUP
S5_STATE = 64
D_FF = 5632
NORM_EPS = 1e-6
GN_EPS = 1e-5

IN_SIZES = (RET_QK_WIDTH, RET_QK_WIDTH, RET_V_WIDTH, RET_V_WIDTH, S5_WIDTH, D_MODEL, D_MODEL)
IN_WIDTH = sum(IN_SIZES)
COL_Q, COL_K, COL_V, COL_GRET, COL_U, COL_GATE_R, COL_GATE_S = (
    sum(IN_SIZES[:i]) for i in range(len(IN_SIZES)))

V7X_VMEM_BYTES = 64 * 1024 * 1024
VMEM_LIMIT = V7X_VMEM_BYTES - 8 * 1024 * 1024

S5_CHUNK = 16
S5_COL_TILE = 256
S5_GROUPS_PER_TILE = S5_COL_TILE // S5_GROUP
S5_STATE_TILE = S5_GROUPS_PER_TILE * S5_STATE * 2
S5_TILES = S5_WIDTH // S5_COL_TILE
RET_BLOCK = 256


def _params(semantics):
    return pltpu.CompilerParams(dimension_semantics=semantics, vmem_limit_bytes=VMEM_LIMIT)


def _sigmoid(x):
    return 1.0 / (1.0 + jnp.exp(-x))


def _mod_kernel(c_ref, w_ref, b_ref, o_ref):
    c = c_ref[...]
    ca = (c * _sigmoid(c)).astype(BF16)
    o_ref[...] = jnp.dot(ca, w_ref[...].astype(BF16), preferred_element_type=F32) + b_ref[...]


def _adaln_mod(c, w_ada, b_ada):
    bn = c.shape[0]
    rows = 8
    c_pad = jnp.zeros((rows, D_MODEL), F32).at[:bn].set(c)
    tn = 1024
    out = pl.pallas_call(
        _mod_kernel,
        grid=(6 * D_MODEL // tn,),
        in_specs=[pl.BlockSpec((rows, D_MODEL), lambda j: (0, 0)),
                  pl.BlockSpec((D_MODEL, tn), lambda j: (0, j)),
                  pl.BlockSpec((1, tn), lambda j: (0, j))],
        out_specs=pl.BlockSpec((rows, tn), lambda j: (0, j)),
        out_shape=jax.ShapeDtypeStruct((rows, 6 * D_MODEL), F32),
        compiler_params=_params(("arbitrary",)),
        name="mod",
    )(c_pad, w_ada, b_ada.reshape(1, -1))
    return out[:bn].reshape(bn, 6, D_MODEL)


def _inproj_kernel(x_ref, mod_ref, g_ref, w_ref, cos_ref, sin_ref, o_ref, h_ref):
    j = pl.program_id(1)

    @pl.when(j == 0)
    def _():
        x = x_ref[...]
        ms = jnp.mean(x * x, axis=-1, keepdims=True)
        y = x * lax.rsqrt(ms + NORM_EPS) * g_ref[...]
        h = y * (1.0 + mod_ref[0, 1:2, :]) + mod_ref[0, 0:1, :]
        h_ref[...] = h.astype(BF16)

    acc = jnp.dot(h_ref[...], w_ref[...], preferred_element_type=F32)
    tn = acc.shape[1]
    q_tiles = COL_K // tn

    @pl.when(j < COL_V // tn)
    def _():
        scale = jnp.where(j >= q_tiles, RET_QK_DIM ** -0.5, 1.0).astype(F32)
        cosf = cos_ref[...]
        sinf = sin_ref[...]
        for hh in range(tn // RET_QK_DIM):
            sl = slice(hh * RET_QK_DIM, (hh + 1) * RET_QK_DIM)
            t = acc[:, sl]
            r = t * cosf + pltpu.roll(t, RET_QK_DIM // 2, 1) * sinf
            o_ref[:, sl] = (r * scale).astype(BF16)

    is_plain = ((j >= COL_V // tn) & (j < COL_GRET // tn)) | ((j >= COL_U // tn) & (j < COL_GATE_R // tn))

    @pl.when(is_plain)
    def _():
        o_ref[...] = acc.astype(BF16)

    @pl.when((j >= COL_GRET // tn) & (j < COL_U // tn))
    def _():
        o_ref[...] = (acc * _sigmoid(acc)).astype(BF16)

    @pl.when(j >= COL_GATE_R // tn)
    def _():
        o_ref[...] = _sigmoid(acc).astype(BF16)


def _inproj(x2, mod, gain, w_in, cosf, sinf, seq):
    t = x2.shape[0]
    tm, tn = 1024, 1024
    tiles_per_seq = seq // tm
    return pl.pallas_call(
        _inproj_kernel,
        grid=(t // tm, IN_WIDTH // tn),
        in_specs=[pl.BlockSpec((tm, D_MODEL), lambda m, j: (m, 0)),
                  pl.BlockSpec((1, 6, D_MODEL), lambda m, j: (m // tiles_per_seq, 0, 0)),
                  pl.BlockSpec((1, D_MODEL), lambda m, j: (0, 0)),
                  pl.BlockSpec((D_MODEL, tn), lambda m, j: (0, j)),
                  pl.BlockSpec((tm, RET_QK_DIM), lambda m, j: (m % tiles_per_seq, 0)),
                  pl.BlockSpec((tm, RET_QK_DIM), lambda m, j: (m % tiles_per_seq, 0))],
        out_specs=pl.BlockSpec((tm, tn), lambda m, j: (m, j)),
        out_shape=jax.ShapeDtypeStruct((t, IN_WIDTH), BF16),
        scratch_shapes=[pltpu.VMEM((tm, D_MODEL), BF16)],
        compiler_params=_params(("parallel", "arbitrary")),
        name="inproj",
    )(x2, mod, gain, w_in, cosf, sinf)


def _ret_kernel(q_ref, k_ref, v_ref, g_ref, o_ref, state_ref):
    n = pl.program_id(1)
    cr = q_ref.shape[0]

    @pl.when(n == 0)
    def _():
        state_ref[...] = jnp.zeros_like(state_ref)

    row = lax.broadcasted_iota(jnp.int32, (cr, cr), 0)
    col = lax.broadcasted_iota(jnp.int32, (cr, cr), 1)
    rel = (row - col).astype(F32)
    idx = lax.broadcasted_iota(jnp.int32, (cr, 1), 0).astype(F32)

    for h in range(RET_HEADS):
        log_g = math.log1p(-(2.0 ** (-5.0 - h)))
        decay = jnp.where(rel >= 0.0, jnp.exp(jnp.maximum(rel, 0.0) * log_g), 0.0)
        w_state = jnp.exp((cr - 1.0 - idx) * log_g)
        w_cross = jnp.exp((idx + 1.0) * log_g)
        chunk_decay = math.exp(cr * log_g)

        qs = slice(h * RET_QK_DIM, (h + 1) * RET_QK_DIM)
        vs = slice(h * RET_V_DIM, (h + 1) * RET_V_DIM)
        q = q_ref[:, qs]
        k = k_ref[:, qs]
        v = v_ref[:, vs]
        scores = lax.dot_general(q, k, (((1,), (1,)), ((), ())), preferred_element_type=F32)
        inner = jnp.dot((scores * decay).astype(BF16), v, preferred_element_type=F32)
        state = state_ref[h]
        cross = jnp.dot(q, state.astype(BF16), preferred_element_type=F32) * w_cross
        ret = inner + cross
        mu = jnp.mean(ret, axis=-1, keepdims=True)
        dev = ret - mu
        var = jnp.mean(dev * dev, axis=-1, keepdims=True)
        normed = dev * lax.rsqrt(var + GN_EPS)
        o_ref[:, vs] = (normed * g_ref[:, vs].astype(F32)).astype(BF16)

        kw = (k.astype(F32) * w_state).astype(BF16)
        kv = lax.dot_general(kw, v, (((0,), (0,)), ((), ())), preferred_element_type=F32)
        state_ref[h] = chunk_decay * state + kv


def _retention(proj, batch, seq):
    t = proj.shape[0]
    cr = RET_BLOCK
    nblk = seq // cr
    return pl.pallas_call(
        _ret_kernel,
        grid=(batch, nblk),
        in_specs=[pl.BlockSpec((cr, RET_QK_WIDTH), lambda b, n: (b * nblk + n, COL_Q // RET_QK_WIDTH)),
                  pl.BlockSpec((cr, RET_QK_WIDTH), lambda b, n: (b * nblk + n, COL_K // RET_QK_WIDTH)),
                  pl.BlockSpec((cr, RET_V_WIDTH), lambda b, n: (b * nblk + n, COL_V // RET_V_WIDTH)),
                  pl.BlockSpec((cr, RET_V_WIDTH), lambda b, n: (b * nblk + n, COL_GRET // RET_V_WIDTH))],
        out_specs=pl.BlockSpec((cr, RET_V_WIDTH), lambda b, n: (b * nblk + n, 0)),
        out_shape=jax.ShapeDtypeStruct((t, RET_V_WIDTH), BF16),
        scratch_shapes=[pltpu.VMEM((RET_HEADS, RET_QK_DIM, RET_V_DIM), F32)],
        compiler_params=_params(("parallel", "arbitrary")),
        name="ret",
    )(proj, proj, proj, proj)


def _cmul(ar, ai, br, bi):
    return ar * br - ai * bi, ar * bi + ai * br


def _s5prep_kernel(are_ref, aim_ref, ldt_ref, bre_ref, bim_ref, cre_ref, cim_ref, d_ref,
                   er_ref, ei_ref, fr_ref, fi_ref, k_ref, lr_ref, li_ref):
    ar = are_ref[...]
    ai = aim_ref[...]
    dt = jnp.exp(ldt_ref[...])
    mag = jnp.exp(ar * dt)
    lbr = mag * jnp.cos(ai * dt)
    lbi = mag * jnp.sin(ai * dt)
    nr = lbr - 1.0
    den = ar * ar + ai * ai
    f_re = (nr * ar + lbi * ai) / den
    f_im = (lbi * ar - nr * ai) / den
    bbr, bbi = _cmul(f_re, f_im, bre_ref[...], bim_ref[...])

    rows = S5_CHUNK * S5_GROUP
    lag = (lax.broadcasted_iota(jnp.int32, (ar.shape[0], rows, ar.shape[2]), 1) // S5_GROUP).astype(F32)

    def lam_pow(k):
        m = jnp.exp(ar * dt * k)
        return m * jnp.cos(ai * dt * k), m * jnp.sin(ai * dt * k)

    def tile_rows(a):
        return jnp.concatenate([a] * S5_CHUNK, axis=1)

    p0r, p0i = lam_pow(lag)
    er, ei = _cmul(p0r, p0i, tile_rows(bbr), tile_rows(bbi))
    er_ref[...] = er
    ei_ref[...] = ei
    p1r, p1i = lam_pow(lag + 1.0)
    fr, fi = _cmul(p1r, p1i, tile_rows(cre_ref[...]), tile_rows(cim_ref[...]))
    fr_ref[...] = fr
    fi_ref[...] = fi

    dims = (((2,), (2,)), ((0,), (0,)))
    kern = (lax.dot_general(cre_ref[...], er, dims, preferred_element_type=F32,
                            precision=lax.Precision.HIGHEST)
            - lax.dot_general(cim_ref[...], ei, dims, preferred_element_type=F32,
                              precision=lax.Precision.HIGHEST))
    hrow = lax.broadcasted_iota(jnp.int32, (1, S5_GROUP, rows), 1)
    ncol = lax.broadcasted_iota(jnp.int32, (1, S5_GROUP, rows), 2)
    k_ref[...] = kern + jnp.where(hrow == ncol, d_ref[...], 0.0)

    lcr, lci = lam_pow(float(S5_CHUNK))
    lr_ref[...] = lcr
    li_ref[...] = lci


def _s5_prep(a_re, a_im, log_dt, b_re, b_im, c_re, c_im, d_skip):
    g, p, hg = S5_GROUPS, S5_STATE, S5_GROUP
    gb = 8
    rows = S5_CHUNK * hg
    gp = lambda: pl.BlockSpec((gb, 1, p), lambda i: (i, 0, 0))
    ghp = lambda: pl.BlockSpec((gb, hg, p), lambda i: (i, 0, 0))
    big = lambda: pl.BlockSpec((gb, rows, p), lambda i: (i, 0, 0))
    outs = pl.pallas_call(
        _s5prep_kernel,
        grid=(g // gb,),
        in_specs=[gp(), gp(), gp(), ghp(), ghp(), ghp(), ghp(),
                  pl.BlockSpec((gb, hg, 1), lambda i: (i, 0, 0))],
        out_specs=[big(), big(), big(), big(),
                   pl.BlockSpec((gb, hg, rows), lambda i: (i, 0, 0)), gp(), gp()],
        out_shape=[jax.ShapeDtypeStruct((g, rows, p), F32)] * 4
        + [jax.ShapeDtypeStruct((g, hg, rows), F32)]
        + [jax.ShapeDtypeStruct((g, 1, p), F32)] * 2,
        compiler_params=_params(("arbitrary",)),
        name="s5prep",
    )(a_re[:, None, :], a_im[:, None, :],
      jnp.broadcast_to(log_dt[:, None, None], (g, 1, p)),
      jnp.swapaxes(b_re, 1, 2), jnp.swapaxes(b_im, 1, 2), c_re, c_im, d_skip[:, :, None])
    return outs


def _s5_weights(er, ei, fr, fi, kern, lcr, lci):
    nt, gl, hg, p, c = S5_TILES, S5_GROUPS_PER_TILE, S5_GROUP, S5_STATE, S5_CHUNK
    eye = jnp.eye(gl, dtype=F32)
    e = jnp.stack([er, ei]).reshape(2, nt, gl, c, hg, p)
    e = jnp.flip(e.transpose(3, 1, 2, 4, 0, 5), axis=0)
    w_in = e[:, :, :, :, :, None, :] * eye[None, None, :, None, None, :, None]
    w_in = w_in.reshape(c, nt, gl * hg, 2 * gl * p).astype(BF16)
    f = jnp.stack([fr, -fi]).reshape(2, nt, gl, c, hg, p)
    f = f.transpose(3, 1, 0, 2, 5, 4)
    w_out = f[:, :, :, :, :, None, :] * eye[None, None, None, :, None, :, None]
    w_out = w_out.reshape(c, nt, 2 * gl * p, gl * hg).astype(BF16)
    k = kern.reshape(nt, gl, hg, c, hg).transpose(3, 0, 1, 4, 2)
    w_lag = k[:, :, :, :, None, :] * eye[None, None, :, None, :, None]
    w_lag = w_lag.reshape(c, nt, gl * hg, gl * hg).astype(BF16)
    lam_c = jnp.stack([lcr.reshape(nt, gl * p), lci.reshape(nt, gl * p)], axis=1)
    return w_in, w_out, w_lag, lam_c


def _s5state_kernel(u_ref, w_ref, lam_ref, o_ref, acc_ref, *, batch, nchunk):
    i = pl.program_id(1)
    part = jnp.dot(u_ref[0], w_ref[0, 0], preferred_element_type=F32)

    @pl.when(i == 0)
    def _():
        acc_ref[...] = part

    @pl.when(i > 0)
    def _():
        acc_ref[...] += part

    @pl.when(i == pl.num_programs(1) - 1)
    def _():
        half = acc_ref.shape[1] // 2
        lr = lam_ref[0, 0:1, :]
        li = lam_ref[0, 1:2, :]

        def step(n, carry):
            new = []
            for b in range(batch):
                xr, xi = carry[2 * b], carry[2 * b + 1]
                r = b * nchunk + n
                sr = acc_ref[pl.ds(r, 1), 0:half]
                si = acc_ref[pl.ds(r, 1), half:2 * half]
                acc_ref[pl.ds(r, 1), 0:half] = xr
                acc_ref[pl.ds(r, 1), half:2 * half] = xi
                new.append(lr * xr - li * xi + sr)
                new.append(lr * xi + li * xr + si)
            return tuple(new)

        zero = jnp.zeros((1, half), F32)
        lax.fori_loop(0, nchunk, step, (zero,) * (2 * batch))
        o_ref[...] = acc_ref[...].astype(BF16)


def _s5_state(u_im, w_in, lam_c, batch):
    c, rows, _ = u_im.shape
    return pl.pallas_call(
        functools.partial(_s5state_kernel, batch=batch, nchunk=rows // batch),
        grid=(S5_TILES, c),
        in_specs=[pl.BlockSpec((1, rows, S5_COL_TILE), lambda ct, i: (i, 0, ct)),
                  pl.BlockSpec((1, 1, S5_COL_TILE, S5_STATE_TILE), lambda ct, i: (i, ct, 0, 0)),
                  pl.BlockSpec((1, 2, S5_STATE_TILE // 2), lambda ct, i: (ct, 0, 0))],
        out_specs=pl.BlockSpec((rows, S5_STATE_TILE), lambda ct, i: (0, ct)),
        out_shape=jax.ShapeDtypeStruct((rows, S5_TILES * S5_STATE_TILE), BF16),
        scratch_shapes=[pltpu.VMEM((rows, S5_STATE_TILE), F32)],
        compiler_params=_params(("parallel", "arbitrary")),
        name="s5state",
    )(u_im, w_in, lam_c)


def _s5out_kernel(u_ref, x_ref, wout_ref, wlag_ref, o_ref, acc_ref):
    i = pl.program_id(1)
    acc_ref[...] = jnp.dot(x_ref[...], wout_ref[0, 0], preferred_element_type=F32)

    def lag_step(j, carry):
        acc_ref[...] += jnp.dot(u_ref[i - j], wlag_ref[j, 0], preferred_element_type=F32)
        return carry

    lax.fori_loop(0, i + 1, lag_step, 0)
    o_ref[0] = jax.nn.gelu(acc_ref[...]).astype(BF16)


def _s5_out(u_im, x_prev, w_out, w_lag):
    c, rows, width = u_im.shape
    return pl.pallas_call(
        _s5out_kernel,
        grid=(S5_TILES, c),
        in_specs=[pl.BlockSpec((c, rows, S5_COL_TILE), lambda ct, i: (0, 0, ct)),
                  pl.BlockSpec((rows, S5_STATE_TILE), lambda ct, i: (0, ct)),
                  pl.BlockSpec((1, 1, S5_STATE_TILE, S5_COL_TILE), lambda ct, i: (i, ct, 0, 0)),
                  pl.BlockSpec((c, 1, S5_COL_TILE, S5_COL_TILE), lambda ct, i: (0, ct, 0, 0))],
        out_specs=pl.BlockSpec((1, rows, S5_COL_TILE), lambda ct, i: (i, 0, ct)),
        out_shape=jax.ShapeDtypeStruct((c, rows, width), BF16),
        scratch_shapes=[pltpu.VMEM((rows, S5_COL_TILE), F32)],
        compiler_params=_params(("parallel", "arbitrary")),
        name="s5out",
    )(u_im, x_prev, w_out, w_lag)


def _merge_kernel(ret_ref, ssm_ref, wro_ref, wga_ref, wgb_ref, gr_ref, gs_ref, o_ref):
    y_ret = jnp.dot(ret_ref[...], wro_ref[...], preferred_element_type=F32)
    ssm = ssm_ref[...]
    glu_a = jnp.dot(ssm, wga_ref[...], preferred_element_type=F32)
    glu_b = jnp.dot(ssm, wgb_ref[...], preferred_element_type=F32)
    y_s5 = glu_a * _sigmoid(glu_b)
    merged = gr_ref[...].astype(F32) * y_ret + gs_ref[...].astype(F32) * y_s5
    o_ref[...] = merged.astype(BF16)


def _merge(retn, yssm, proj, w_ret_out, w_s5_glu):
    t = retn.shape[0]
    tm, tn = 1024, 512
    nb = D_MODEL // tn
    return pl.pallas_call(
        _merge_kernel,
        grid=(t // tm, nb),
        in_specs=[pl.BlockSpec((tm, RET_V_WIDTH), lambda m, j: (m, 0)),
                  pl.BlockSpec((tm, S5_WIDTH), lambda m, j: (m, 0)),
                  pl.BlockSpec((RET_V_WIDTH, tn), lambda m, j: (0, j)),
                  pl.BlockSpec((S5_WIDTH, tn), lambda m, j: (0, j)),
                  pl.BlockSpec((S5_WIDTH, tn), lambda m, j: (0, nb + j)),
                  pl.BlockSpec((tm, tn), lambda m, j: (m, COL_GATE_R // tn + j)),
                  pl.BlockSpec((tm, tn), lambda m, j: (m, COL_GATE_S // tn + j))],
        out_specs=pl.BlockSpec((tm, tn), lambda m, j: (m, j)),
        out_shape=jax.ShapeDtypeStruct((t, D_MODEL), BF16),
        compiler_params=_params(("parallel", "arbitrary")),
        name="merge",
    )(retn, yssm, w_ret_out, w_s5_glu, w_s5_glu, proj, proj)


def _oproj_kernel(a_ref, w_ref, x_ref, mod_ref, g_ref, o_ref):
    y = jnp.dot(a_ref[...], w_ref[...], preferred_element_type=F32)
    ms = jnp.mean(y * y, axis=-1, keepdims=True)
    yn = y * lax.rsqrt(ms + NORM_EPS) * g_ref[...]
    o_ref[...] = x_ref[...] + mod_ref[0, 2:3, :] * yn


def _oproj(merged, w_out, x2, mod, gain, seq):
    t = merged.shape[0]
    tm = 512
    tiles_per_seq = seq // tm
    return pl.pallas_call(
        _oproj_kernel,
        grid=(t // tm,),
        in_specs=[pl.BlockSpec((tm, D_MODEL), lambda m: (m, 0)),
                  pl.BlockSpec((D_MODEL, D_MODEL), lambda m: (0, 0)),
                  pl.BlockSpec((tm, D_MODEL), lambda m: (m, 0)),
                  pl.BlockSpec((1, 6, D_MODEL), lambda m: (m // tiles_per_seq, 0, 0)),
                  pl.BlockSpec((1, D_MODEL), lambda m: (0, 0))],
        out_specs=pl.BlockSpec((tm, D_MODEL), lambda m: (m, 0)),
        out_shape=jax.ShapeDtypeStruct((t, D_MODEL), F32),
        compiler_params=_params(("parallel",)),
        name="oproj",
    )(merged, w_out, x2, mod, gain)


def _ffn_kernel(x_ref, mod_ref, g2_ref, g3_ref, wa_ref, wb_ref, wo_ref, o_ref, h_ref, acc_ref):
    j = pl.program_id(1)

    @pl.when(j == 0)
    def _():
        x = x_ref[...]
        ms = jnp.mean(x * x, axis=-1, keepdims=True)
        y = x * lax.rsqrt(ms + NORM_EPS) * g2_ref[...]
        h_ref[...] = (y * (1.0 + mod_ref[0, 4:5, :]) + mod_ref[0, 3:4, :]).astype(BF16)

    h = h_ref[...]
    a = jnp.dot(h, wa_ref[...], preferred_element_type=F32)
    b = jnp.dot(h, wb_ref[...], preferred_element_type=F32)
    act = (a * _sigmoid(a) * b).astype(BF16)
    part = jnp.dot(act, wo_ref[...], preferred_element_type=F32)

    @pl.when(j == 0)
    def _():
        acc_ref[...] = part

    @pl.when(j > 0)
    def _():
        acc_ref[...] += part

    @pl.when(j == pl.num_programs(1) - 1)
    def _():
        y = acc_ref[...]
        ms = jnp.mean(y * y, axis=-1, keepdims=True)
        yn = y * lax.rsqrt(ms + NORM_EPS) * g3_ref[...]
        o_ref[...] = x_ref[...] + mod_ref[0, 5:6, :] * yn


def _ffn(x1, mod, gain2, gain3, w_ffn_in, w_ffn_out, seq):
    t = x1.shape[0]
    tm, tf = 512, 512
    nf = D_FF // tf
    tiles_per_seq = seq // tm
    return pl.pallas_call(
        _ffn_kernel,
        grid=(t // tm, nf),
        in_specs=[pl.BlockSpec((tm, D_MODEL), lambda m, j: (m, 0)),
                  pl.BlockSpec((1, 6, D_MODEL), lambda m, j: (m // tiles_per_seq, 0, 0)),
                  pl.BlockSpec((1, D_MODEL), lambda m, j: (0, 0)),
                  pl.BlockSpec((1, D_MODEL), lambda m, j: (0, 0)),
                  pl.BlockSpec((D_MODEL, tf), lambda m, j: (0, j)),
                  pl.BlockSpec((D_MODEL, tf), lambda m, j: (0, nf + j)),
                  pl.BlockSpec((tf, D_MODEL), lambda m, j: (j, 0))],
        out_specs=pl.BlockSpec((tm, D_MODEL), lambda m, j: (m, 0)),
        out_shape=jax.ShapeDtypeStruct((t, D_MODEL), F32),
        scratch_shapes=[pltpu.VMEM((tm, D_MODEL), BF16), pltpu.VMEM((tm, D_MODEL), F32)],
        compiler_params=_params(("parallel", "arbitrary")),
        name="ffn",
    )(x1, mod, gain2, gain3, w_ffn_in, w_ffn_in, w_ffn_out)


def _rope_tables(seq):
    pos = jnp.arange(seq, dtype=F32)
    inv_freq = ROPE_BASE ** (-jnp.arange(RET_QK_DIM // 2, dtype=F32) * (2.0 / RET_QK_DIM))
    ang = pos[:, None] * inv_freq[None, :]
    cos, sin = jnp.cos(ang), jnp.sin(ang)
    return jnp.concatenate([cos, cos], axis=-1), jnp.concatenate([-sin, sin], axis=-1)


def _s5_branch(proj, params, batch, seq):
    a_re, a_im, log_dt, b_re, b_im, c_re, c_im, d_skip = params
    t = proj.shape[0]
    rows = t // S5_CHUNK
    w_in, w_out, w_lag, lam_c = _s5_weights(*_s5_prep(a_re, a_im, log_dt, b_re, b_im, c_re, c_im, d_skip))
    u = proj[:, COL_U:COL_U + S5_WIDTH]
    u_im = u.reshape(rows, S5_CHUNK, S5_WIDTH).transpose(1, 0, 2)
    x_prev = _s5_state(u_im, w_in, lam_c, batch)
    y_im = _s5_out(u_im, x_prev, w_out, w_lag)
    return y_im.transpose(1, 0, 2).reshape(t, S5_WIDTH)


def kernel(x, c, w_ada, b_ada, norm_gains, w_in, w_ret_out, ssm_a_re, ssm_a_im, ssm_log_dt,
           ssm_b_re, ssm_b_im, ssm_c_re, ssm_c_im, ssm_d, w_s5_glu, w_out, w_ffn_in, w_ffn_out):
    batch, seq, _ = x.shape
    depth = w_in.shape[0]
    cosf, sinf = _rope_tables(seq)
    xt = x.reshape(batch * seq, D_MODEL)
    for l in range(depth):
        gains = norm_gains[l][:, None, :]
        mod = _adaln_mod(c, w_ada[l], b_ada[l])
        proj = _inproj(xt, mod, gains[0], w_in[l].astype(BF16), cosf, sinf, seq)
        retn = _retention(proj, batch, seq)
        yssm = _s5_branch(proj, (ssm_a_re[l], ssm_a_im[l], ssm_log_dt[l], ssm_b_re[l], ssm_b_im[l],
                                 ssm_c_re[l], ssm_c_im[l], ssm_d[l]), batch, seq)
        merged = _merge(retn, yssm, proj, w_ret_out[l].astype(BF16), w_s5_glu[l].astype(BF16))
        x1 = _oproj(merged, w_out[l].astype(BF16), xt, mod, gains[1], seq)
        xt = _ffn(x1, mod, gains[2], gains[3], w_ffn_in[l].astype(BF16), w_ffn_out[l].astype(BF16), seq)
    return xt.reshape(batch, seq, D_MODEL)
```

```python
import functools
import math

import jax
import jax.numpy as jnp
from jax import lax
from jax.experimental import pallas as pl
from jax.experimental.pallas import tpu as pltpu

F32 = jnp.float32
BF16 = jnp.bfloat16

D_MODEL = 2048
RET_HEADS = 8
RET_QK_DIM = 128
RET_V_DIM = 256
RET_QK_WIDTH = RET_HEADS * RET_QK_DIM
RET_V_WIDTH = RET_HEADS * RET_V_DIM
ROPE_BASE = 10000.0
S5_GROUP = 16
S5_WIDTH = D_MODEL // 2
S5_GROUPS = S5_WIDTH // S5_GROUP
S5_STATE = 64
D_FF = 5632
NORM_EPS = 1e-6
GN_EPS = 1e-5

IN_SIZES = (RET_QK_WIDTH, RET_QK_WIDTH, RET_V_WIDTH, RET_V_WIDTH, S5_WIDTH, D_MODEL, D_MODEL)
IN_WIDTH = sum(IN_SIZES)
COL_Q, COL_K, COL_V, COL_GRET, COL_U, COL_GATE_R, COL_GATE_S = (
    sum(IN_SIZES[:i]) for i in range(len(IN_SIZES)))

V7X_VMEM_BYTES = 64 * 1024 * 1024
VMEM_LIMIT = V7X_VMEM_BYTES - 8 * 1024 * 1024
MXU_COLS = 256

S5_CHUNK = 16
S5_COL_TILE = 128
S5_GROUPS_PER_TILE = S5_COL_TILE // S5_GROUP
S5_GROUP_STATE = 2 * S5_STATE
S5_STATE_TILE = S5_GROUPS_PER_TILE * S5_GROUP_STATE
S5_TILES = S5_WIDTH // S5_COL_TILE
RET_BLOCK = 256


def _params(semantics):
    return pltpu.CompilerParams(dimension_semantics=semantics, vmem_limit_bytes=VMEM_LIMIT)


def _sigmoid(x):
    return 1.0 / (1.0 + jnp.exp(-x))


def _mod_kernel(c_ref, w_ref, b_ref, o_ref):
    @pl.when(pl.program_id(0) == 0)
    def _():
        o_ref[...] = jnp.broadcast_to(b_ref[...], o_ref.shape)

    c = c_ref[...]
    ca = (c * _sigmoid(c)).astype(BF16)
    o_ref[...] += jnp.dot(ca, w_ref[...].astype(BF16), preferred_element_type=F32)


def _adaln_mod(c, w_ada, b_ada):
    bn = c.shape[0]
    rows = 8
    c_pad = jnp.zeros((rows, D_MODEL), F32).at[:bn].set(c)
    tk = 256
    out = pl.pallas_call(
        _mod_kernel,
        grid=(D_MODEL // tk,),
        in_specs=[pl.BlockSpec((rows, tk), lambda k: (0, k)),
                  pl.BlockSpec((tk, 6 * D_MODEL), lambda k: (k, 0)),
                  pl.BlockSpec((1, 6 * D_MODEL), lambda k: (0, 0))],
        out_specs=pl.BlockSpec((rows, 6 * D_MODEL), lambda k: (0, 0)),
        out_shape=jax.ShapeDtypeStruct((rows, 6 * D_MODEL), F32),
        compiler_params=_params(("arbitrary",)),
        name="mod",
    )(c_pad, w_ada, b_ada.reshape(1, -1))
    return out[:bn].reshape(bn, 6, D_MODEL)


def _inproj_kernel(x_ref, mod_ref, g_ref, w_ref, cos_ref, sin_ref, o_ref, h_ref):
    j = pl.program_id(1)

    @pl.when(j == 0)
    def _():
        x = x_ref[...]
        ms = jnp.mean(x * x, axis=-1, keepdims=True)
        y = x * lax.rsqrt(ms + NORM_EPS) * g_ref[...]
        h = y * (1.0 + mod_ref[0, 1:2, :]) + mod_ref[0, 0:1, :]
        h_ref[...] = h.astype(BF16)

    tn = w_ref.shape[1]
    q_tiles = COL_K // tn

    def project(epilogue):
        for k in range(tn // MXU_COLS):
            sl = slice(k * MXU_COLS, (k + 1) * MXU_COLS)
            acc = jnp.dot(h_ref[...], w_ref[:, sl], preferred_element_type=F32)
            o_ref[:, sl] = epilogue(acc).astype(BF16)

    @pl.when(j < COL_V // tn)
    def _():
        scale = jnp.where(j >= q_tiles, RET_QK_DIM ** -0.5, 1.0).astype(F32)
        cosf = cos_ref[...] * scale
        sinf = sin_ref[...] * scale

        def rope(acc):
            heads = []
            for hh in range(MXU_COLS // RET_QK_DIM):
                t = acc[:, hh * RET_QK_DIM:(hh + 1) * RET_QK_DIM]
                heads.append(t * cosf + pltpu.roll(t, RET_QK_DIM // 2, 1) * sinf)
            return jnp.concatenate(heads, axis=1)

        project(rope)

    is_plain = ((j >= COL_V // tn) & (j < COL_GRET // tn)) | ((j >= COL_U // tn) & (j < COL_GATE_R // tn))

    @pl.when(is_plain)
    def _():
        project(lambda acc: acc)

    @pl.when((j >= COL_GRET // tn) & (j < COL_U // tn))
    def _():
        project(lambda acc: acc * _sigmoid(acc))

    @pl.when(j >= COL_GATE_R // tn)
    def _():
        project(_sigmoid)


def _inproj(x2, mod, gain, w_in, cosf, sinf, seq):
    t = x2.shape[0]
    tm, tn = 1024, 1024
    tiles_per_seq = seq // tm
    return pl.pallas_call(
        _inproj_kernel,
        grid=(t // tm, IN_WIDTH // tn),
        in_specs=[pl.BlockSpec((tm, D_MODEL), lambda m, j: (m, 0)),
                  pl.BlockSpec((1, 6, D_MODEL), lambda m, j: (m // tiles_per_seq, 0, 0)),
                  pl.BlockSpec((1, D_MODEL), lambda m, j: (0, 0)),
                  pl.BlockSpec((D_MODEL, tn), lambda m, j: (0, j)),
                  pl.BlockSpec((tm, RET_QK_DIM), lambda m, j: (m % tiles_per_seq, 0)),
                  pl.BlockSpec((tm, RET_QK_DIM), lambda m, j: (m % tiles_per_seq, 0))],
        out_specs=pl.BlockSpec((tm, tn), lambda m, j: (m, j)),
        out_shape=jax.ShapeDtypeStruct((t, IN_WIDTH), BF16),
        scratch_shapes=[pltpu.VMEM((tm, D_MODEL), BF16)],
        compiler_params=_params(("parallel", "arbitrary")),
        name="inproj",
    )(x2, mod, gain, w_in, cosf, sinf)


def _ret_kernel(q_ref, k_ref, v_ref, g_ref, o_ref, state_ref):
    n = pl.program_id(1)
    cr = q_ref.shape[0]

    @pl.when(n == 0)
    def _():
        state_ref[...] = jnp.zeros_like(state_ref)

    row = lax.broadcasted_iota(jnp.int32, (cr, cr), 0)
    col = lax.broadcasted_iota(jnp.int32, (cr, cr), 1)
    rel = (row - col).astype(F32)
    idx = lax.broadcasted_iota(jnp.int32, (cr, 1), 0).astype(F32)

    for h in range(RET_HEADS):
        log_g = math.log1p(-(2.0 ** (-5.0 - h)))
        decay = jnp.where(rel >= 0.0, jnp.exp(jnp.maximum(rel, 0.0) * log_g), 0.0)
        w_state = jnp.exp((cr - 1.0 - idx) * log_g)
        w_cross = jnp.exp((idx + 1.0) * log_g)
        chunk_decay = math.exp(cr * log_g)

        qs = slice(h * RET_QK_DIM, (h + 1) * RET_QK_DIM)
        vs = slice(h * RET_V_DIM, (h + 1) * RET_V_DIM)
        q = q_ref[:, qs]
        k = k_ref[:, qs]
        v = v_ref[:, vs]
        scores = lax.dot_general(q, k, (((1,), (1,)), ((), ())), preferred_element_type=F32)
        inner = jnp.dot((scores * decay).astype(BF16), v, preferred_element_type=F32)
        state = state_ref[h]
        cross = jnp.dot(q, state.astype(BF16), preferred_element_type=F32) * w_cross
        ret = inner + cross
        mu = jnp.mean(ret, axis=-1, keepdims=True)
        dev = ret - mu
        var = jnp.mean(dev * dev, axis=-1, keepdims=True)
        normed = dev * lax.rsqrt(var + GN_EPS)
        o_ref[:, vs] = (normed * g_ref[:, vs].astype(F32)).astype(BF16)

        kw = (k.astype(F32) * w_state).astype(BF16)
        kv = lax.dot_general(kw, v, (((0,), (0,)), ((), ())), preferred_element_type=F32)
        state_ref[h] = chunk_decay * state + kv


def _retention(proj, batch, seq):
    t = proj.shape[0]
    cr = RET_BLOCK
    nblk = seq // cr
    return pl.pallas_call(
        _ret_kernel,
        grid=(batch, nblk),
        in_specs=[pl.BlockSpec((cr, RET_QK_WIDTH), lambda b, n: (b * nblk + n, COL_Q // RET_QK_WIDTH)),
                  pl.BlockSpec((cr, RET_QK_WIDTH), lambda b, n: (b * nblk + n, COL_K // RET_QK_WIDTH)),
                  pl.BlockSpec((cr, RET_V_WIDTH), lambda b, n: (b * nblk + n, COL_V // RET_V_WIDTH)),
                  pl.BlockSpec((cr, RET_V_WIDTH), lambda b, n: (b * nblk + n, COL_GRET // RET_V_WIDTH))],
        out_specs=pl.BlockSpec((cr, RET_V_WIDTH), lambda b, n: (b * nblk + n, 0)),
        out_shape=jax.ShapeDtypeStruct((t, RET_V_WIDTH), BF16),
        scratch_shapes=[pltpu.VMEM((RET_HEADS, RET_QK_DIM, RET_V_DIM), F32)],
        compiler_params=_params(("parallel", "arbitrary")),
        name="ret",
    )(proj, proj, proj, proj)


def _cmul(ar, ai, br, bi):
    return ar * br - ai * bi, ar * bi + ai * br


def _s5prep_kernel(are_ref, aim_ref, ldt_ref, bre_ref, bim_ref, cre_ref, cim_ref, d_ref,
                   er_ref, ei_ref, fr_ref, fi_ref, k_ref, lr_ref, li_ref):
    ar = are_ref[...]
    ai = aim_ref[...]
    dt = jnp.exp(ldt_ref[...])
    mag = jnp.exp(ar * dt)
    lbr = mag * jnp.cos(ai * dt)
    lbi = mag * jnp.sin(ai * dt)
    nr = lbr - 1.0
    den = ar * ar + ai * ai
    f_re = (nr * ar + lbi * ai) / den
    f_im = (lbi * ar - nr * ai) / den
    bbr, bbi = _cmul(f_re, f_im, bre_ref[...], bim_ref[...])

    rows = S5_CHUNK * S5_GROUP
    lag = (lax.broadcasted_iota(jnp.int32, (ar.shape[0], rows, ar.shape[2]), 1) // S5_GROUP).astype(F32)

    def lam_pow(k):
        m = jnp.exp(ar * dt * k)
        return m * jnp.cos(ai * dt * k), m * jnp.sin(ai * dt * k)

    def tile_rows(a):
        return jnp.concatenate([a] * S5_CHUNK, axis=1)

    p0r, p0i = lam_pow(lag)
    er, ei = _cmul(p0r, p0i, tile_rows(bbr), tile_rows(bbi))
    er_ref[...] = er
    ei_ref[...] = ei
    p1r, p1i = lam_pow(lag + 1.0)
    fr, fi = _cmul(p1r, p1i, tile_rows(cre_ref[...]), tile_rows(cim_ref[...]))
    fr_ref[...] = fr
    fi_ref[...] = -fi

    dims = (((2,), (2,)), ((0,), (0,)))
    kern = (lax.dot_general(cre_ref[...], er, dims, preferred_element_type=F32,
                            precision=lax.Precision.HIGHEST)
            - lax.dot_general(cim_ref[...], ei, dims, preferred_element_type=F32,
                              precision=lax.Precision.HIGHEST))
    hrow = lax.broadcasted_iota(jnp.int32, (1, S5_GROUP, rows), 1)
    ncol = lax.broadcasted_iota(jnp.int32, (1, S5_GROUP, rows), 2)
    k_ref[...] = kern + jnp.where(hrow == ncol, d_ref[...], 0.0)

    lcr, lci = lam_pow(float(S5_CHUNK))
    lr_ref[...] = lcr
    li_ref[...] = lci


def _s5_prep(a_re, a_im, log_dt, b_re, b_im, c_re, c_im, d_skip):
    g, p, hg = S5_GROUPS, S5_STATE, S5_GROUP
    gb = 8
    rows = S5_CHUNK * hg
    gp = lambda: pl.BlockSpec((gb, 1, p), lambda i: (i, 0, 0))
    ghp = lambda: pl.BlockSpec((gb, hg, p), lambda i: (i, 0, 0))
    big = lambda: pl.BlockSpec((gb, rows, p), lambda i: (i, 0, 0))
    outs = pl.pallas_call(
        _s5prep_kernel,
        grid=(g // gb,),
        in_specs=[gp(), gp(), gp(), ghp(), ghp(), ghp(), ghp(),
                  pl.BlockSpec((gb, hg, 1), lambda i: (i, 0, 0))],
        out_specs=[big(), big(), big(), big(),
                   pl.BlockSpec((gb, hg, rows), lambda i: (i, 0, 0)), gp(), gp()],
        out_shape=[jax.ShapeDtypeStruct((g, rows, p), F32)] * 4
        + [jax.ShapeDtypeStruct((g, hg, rows), F32)]
        + [jax.ShapeDtypeStruct((g, 1, p), F32)] * 2,
        compiler_params=_params(("arbitrary",)),
        name="s5prep",
    )(a_re[:, None, :], a_im[:, None, :],
      jnp.broadcast_to(log_dt[:, None, None], (g, 1, p)),
      jnp.swapaxes(b_re, 1, 2), jnp.swapaxes(b_im, 1, 2), c_re, c_im, d_skip[:, :, None])
    return outs


def _s5_compact(er, ei, fr, fin, kern, lcr, lci):
    g, hg, p, c = S5_GROUPS, S5_GROUP, S5_STATE, S5_CHUNK
    e = jnp.concatenate([er, ei], axis=-1).reshape(g, c, hg, 2 * p)
    e_c = jnp.flip(e.transpose(1, 0, 2, 3), axis=0).reshape(c, g * hg, 2 * p).astype(BF16)
    f = jnp.concatenate([fr, fin], axis=-1).reshape(g, c, hg, 2 * p)
    f_c = f.transpose(1, 3, 0, 2).reshape(c, 2 * p, g * hg).astype(BF16)
    k_c = kern.reshape(g, hg, c, hg).transpose(2, 3, 0, 1).reshape(c, hg, g * hg).astype(BF16)
    lam = jnp.stack([jnp.concatenate([lcr, lcr], axis=-1).reshape(S5_TILES, S5_STATE_TILE),
                     jnp.concatenate([lci, lci], axis=-1).reshape(S5_TILES, S5_STATE_TILE)], axis=1)
    return e_c, f_c, k_c, lam


def _s5state_kernel(u_ref, e_ref, lam_ref, o_ref, w_ref, acc_ref, swp_ref, *, batch, nchunk):
    c = u_ref.shape[0]
    ct, gs, hg = S5_COL_TILE, S5_GROUP_STATE, S5_GROUP

    @pl.when(pl.program_id(0) == 0)
    def _():
        w_ref[...] = jnp.zeros_like(w_ref)

    for i in range(c):
        for gl in range(S5_GROUPS_PER_TILE):
            w_ref[i * ct + gl * hg:i * ct + (gl + 1) * hg, gl * gs:(gl + 1) * gs] = (
                e_ref[i, gl * hg:(gl + 1) * hg, :])

    lhs = jnp.concatenate([u_ref[i] for i in range(c)], axis=1)
    s = jnp.dot(lhs, w_ref[...], preferred_element_type=F32)
    width = s.shape[1]
    is_re = (lax.broadcasted_iota(jnp.int32, (1, width), 1) % gs) < S5_STATE
    acc_ref[...] = s
    swp_ref[...] = jnp.where(is_re, pltpu.roll(s, width - S5_STATE, 1), pltpu.roll(s, S5_STATE, 1))

    lr = lam_ref[0, 0:1, :]
    li = lam_ref[0, 1:2, :]
    lx = jnp.where(is_re, -li, li)

    def step(n, carry):
        new = []
        for b in range(batch):
            z, zs = carry[2 * b], carry[2 * b + 1]
            r = b * nchunk + n
            s_row = acc_ref[pl.ds(r, 1), :]
            ss_row = swp_ref[pl.ds(r, 1), :]
            acc_ref[pl.ds(r, 1), :] = z
            new.append(lr * z + lx * zs + s_row)
            new.append(lr * zs - lx * z + ss_row)
        return tuple(new)

    zero = jnp.zeros((1, width), F32)
    lax.fori_loop(0, nchunk, step, (zero,) * (2 * batch))
    o_ref[...] = acc_ref[...].astype(BF16)


def _s5_state(u_im, e_c, lam, batch):
    c, rows, _ = u_im.shape
    return pl.pallas_call(
        functools.partial(_s5state_kernel, batch=batch, nchunk=rows // batch),
        grid=(S5_TILES,),
        in_specs=[pl.BlockSpec((c, rows, S5_COL_TILE), lambda t: (0, 0, t)),
                  pl.BlockSpec((c, S5_COL_TILE, S5_GROUP_STATE), lambda t: (0, t, 0)),
                  pl.BlockSpec((1, 2, S5_STATE_TILE), lambda t: (t, 0, 0))],
        out_specs=pl.BlockSpec((rows, S5_STATE_TILE), lambda t: (0, t)),
        out_shape=jax.ShapeDtypeStruct((rows, S5_TILES * S5_STATE_TILE), BF16),
        scratch_shapes=[pltpu.VMEM((c * S5_COL_TILE, S5_STATE_TILE), BF16),
                        pltpu.VMEM((rows, S5_STATE_TILE), F32),
                        pltpu.VMEM((rows, S5_STATE_TILE), F32)],
        compiler_params=_params(("arbitrary",)),
        name="s5state",
    )(u_im, e_c, lam)


def _s5out_kernel(u_ref, x_ref, f_ref, k_ref, o_ref, w_ref, lhs_ref):
    c = u_ref.shape[0]
    ct, gs, hg, st = S5_COL_TILE, S5_GROUP_STATE, S5_GROUP, S5_STATE_TILE

    @pl.when(pl.program_id(0) == 0)
    def _():
        w_ref[...] = jnp.zeros_like(w_ref)

    lane_group = lax.broadcasted_iota(jnp.int32, (gs, ct), 1) // hg
    for i in range(c):
        f_i = f_ref[i]
        for gl in range(S5_GROUPS_PER_TILE):
            w_ref[gl * gs:(gl + 1) * gs, i * ct:(i + 1) * ct] = jnp.where(
                lane_group == gl, f_i, jnp.zeros_like(f_i))
    lane_group = lax.broadcasted_iota(jnp.int32, (hg, ct), 1) // hg
    for j in range(c):
        k_j = k_ref[j]
        lag_block = jnp.concatenate(
            [jnp.where(lane_group == gl, k_j, jnp.zeros_like(k_j)) for gl in range(S5_GROUPS_PER_TILE)],
            axis=0)
        for i0 in range(c - j):
            w_ref[st + i0 * ct:st + (i0 + 1) * ct, (i0 + j) * ct:(i0 + j + 1) * ct] = lag_block

    lhs_ref[:, 0:st] = x_ref[...]
    for i in range(c):
        lhs_ref[:, st + i * ct:st + (i + 1) * ct] = u_ref[i]

    for n in range(c // 2):
        k = st + 2 * ct * (n + 1)
        acc = jnp.dot(lhs_ref[:, 0:k], w_ref[0:k, 2 * n * ct:2 * (n + 1) * ct], preferred_element_type=F32)
        y = jax.nn.gelu(acc).astype(BF16)
        o_ref[2 * n] = y[:, 0:ct]
        o_ref[2 * n + 1] = y[:, ct:2 * ct]


def _s5_out(u_im, x_prev, f_c, k_c):
    c, rows, width = u_im.shape
    return pl.pallas_call(
        _s5out_kernel,
        grid=(S5_TILES,),
        in_specs=[pl.BlockSpec((c, rows, S5_COL_TILE), lambda t: (0, 0, t)),
                  pl.BlockSpec((rows, S5_STATE_TILE), lambda t: (0, t)),
                  pl.BlockSpec((c, S5_GROUP_STATE, S5_COL_TILE), lambda t: (0, 0, t)),
                  pl.BlockSpec((c, S5_GROUP, S5_COL_TILE), lambda t: (0, 0, t))],
        out_specs=pl.BlockSpec((c, rows, S5_COL_TILE), lambda t: (0, 0, t)),
        out_shape=jax.ShapeDtypeStruct((c, rows, width), BF16),
        scratch_shapes=[pltpu.VMEM((S5_STATE_TILE + c * S5_COL_TILE, c * S5_COL_TILE), BF16),
                        pltpu.VMEM((rows, S5_STATE_TILE + c * S5_COL_TILE), BF16)],
        compiler_params=_params(("arbitrary",)),
        name="s5out",
    )(u_im, x_prev, f_c, k_c)


def _merge_kernel(ret_ref, ssm_ref, wro_ref, wga_ref, wgb_ref, gr_ref, gs_ref, o_ref):
    for k in range(o_ref.shape[1] // MXU_COLS):
        sl = slice(k * MXU_COLS, (k + 1) * MXU_COLS)
        y_ret = jnp.dot(ret_ref[...], wro_ref[:, sl], preferred_element_type=F32)
        glu_a = jnp.dot(ssm_ref[...], wga_ref[:, sl], preferred_element_type=F32)
        glu_b = jnp.dot(ssm_ref[...], wgb_ref[:, sl], preferred_element_type=F32)
        y_s5 = glu_a * _sigmoid(glu_b)
        merged = gr_ref[:, sl].astype(F32) * y_ret + gs_ref[:, sl].astype(F32) * y_s5
        o_ref[:, sl] = merged.astype(BF16)


def _merge(retn, yssm, proj, w_ret_out, w_s5_glu):
    t = retn.shape[0]
    tm, tn = 1024, 512
    nb = D_MODEL // tn
    return pl.pallas_call(
        _merge_kernel,
        grid=(t // tm, nb),
        in_specs=[pl.BlockSpec((tm, RET_V_WIDTH), lambda m, j: (m, 0)),
                  pl.BlockSpec((tm, S5_WIDTH), lambda m, j: (m, 0)),
                  pl.BlockSpec((RET_V_WIDTH, tn), lambda m, j: (0, j)),
                  pl.BlockSpec((S5_WIDTH, tn), lambda m, j: (0, j)),
                  pl.BlockSpec((S5_WIDTH, tn), lambda m, j: (0, nb + j)),
                  pl.BlockSpec((tm, tn), lambda m, j: (m, COL_GATE_R // tn + j)),
                  pl.BlockSpec((tm, tn), lambda m, j: (m, COL_GATE_S // tn + j))],
        out_specs=pl.BlockSpec((tm, tn), lambda m, j: (m, j)),
        out_shape=jax.ShapeDtypeStruct((t, D_MODEL), BF16),
        compiler_params=_params(("parallel", "arbitrary")),
        name="merge",
    )(retn, yssm, w_ret_out, w_s5_glu, w_s5_glu, proj, proj)


def _oproj_kernel(a_ref, w_ref, x_ref, mod_ref, g_ref, o_ref):
    y = jnp.dot(a_ref[...], w_ref[...], preferred_element_type=F32)
    ms = jnp.mean(y * y, axis=-1, keepdims=True)
    yn = y * lax.rsqrt(ms + NORM_EPS) * g_ref[...]
    o_ref[...] = x_ref[...] + mod_ref[0, 2:3, :] * yn


def _oproj(merged, w_out, x2, mod, gain, seq):
    t = merged.shape[0]
    tm = 512
    tiles_per_seq = seq // tm
    return pl.pallas_call(
        _oproj_kernel,
        grid=(t // tm,),
        in_specs=[pl.BlockSpec((tm, D_MODEL), lambda m: (m, 0)),
                  pl.BlockSpec((D_MODEL, D_MODEL), lambda m: (0, 0)),
                  pl.BlockSpec((tm, D_MODEL), lambda m: (m, 0)),
                  pl.BlockSpec((1, 6, D_MODEL), lambda m: (m // tiles_per_seq, 0, 0)),
                  pl.BlockSpec((1, D_MODEL), lambda m: (0, 0))],
        out_specs=pl.BlockSpec((tm, D_MODEL), lambda m: (m, 0)),
        out_shape=jax.ShapeDtypeStruct((t, D_MODEL), F32),
        compiler_params=_params(("parallel",)),
        name="oproj",
    )(merged, w_out, x2, mod, gain)


def _ffn_kernel(x_ref, mod_ref, g2_ref, g3_ref, wa_ref, wb_ref, wo_ref, o_ref, h_ref, act_ref, *, nf, no):
    j = pl.program_id(1)
    tf = wa_ref.shape[1]
    tno = wo_ref.shape[1]

    @pl.when(j == 0)
    def _():
        x = x_ref[...]
        ms = jnp.mean(x * x, axis=-1, keepdims=True)
        y = x * lax.rsqrt(ms + NORM_EPS) * g2_ref[...]
        h_ref[...] = (y * (1.0 + mod_ref[0, 4:5, :]) + mod_ref[0, 3:4, :]).astype(BF16)

    @pl.when(j < nf)
    def _():
        parts = []
        for k in range(tf // MXU_COLS):
            sl = slice(k * MXU_COLS, (k + 1) * MXU_COLS)
            a = jnp.dot(h_ref[...], wa_ref[:, sl], preferred_element_type=F32)
            b = jnp.dot(h_ref[...], wb_ref[:, sl], preferred_element_type=F32)
            parts.append((a * _sigmoid(a) * b).astype(BF16))
        act_ref[j] = jnp.concatenate(parts, axis=1)

    for n in range(no):
        @pl.when(j == nf + n)
        def _(n=n):
            y = jnp.dot(act_ref[0], wo_ref[0:tf, :], preferred_element_type=F32)
            for jj in range(1, nf):
                y += jnp.dot(act_ref[jj], wo_ref[jj * tf:(jj + 1) * tf, :], preferred_element_type=F32)
            o_ref[:, n * tno:(n + 1) * tno] = y

    @pl.when(j == nf + no - 1)
    def _():
        y = o_ref[...]
        ms = jnp.mean(y * y, axis=-1, keepdims=True)
        yn = y * lax.rsqrt(ms + NORM_EPS) * g3_ref[...]
        o_ref[...] = x_ref[...] + mod_ref[0, 5:6, :] * yn


def _ffn(x1, mod, gain2, gain3, w_ffn_in, w_ffn_out, seq):
    t = x1.shape[0]
    tm, tf, tno = 512, 512, 512
    nf = D_FF // tf
    no = D_MODEL // tno
    tiles_per_seq = seq // tm
    return pl.pallas_call(
        functools.partial(_ffn_kernel, nf=nf, no=no),
        grid=(t // tm, nf + no),
        in_specs=[pl.BlockSpec((tm, D_MODEL), lambda m, j: (m, 0)),
                  pl.BlockSpec((1, 6, D_MODEL), lambda m, j: (m // tiles_per_seq, 0, 0)),
                  pl.BlockSpec((1, D_MODEL), lambda m, j: (0, 0)),
                  pl.BlockSpec((1, D_MODEL), lambda m, j: (0, 0)),
                  pl.BlockSpec((D_MODEL, tf), lambda m, j: (0, jnp.minimum(j, nf - 1))),
                  pl.BlockSpec((D_MODEL, tf), lambda m, j: (0, nf + jnp.minimum(j, nf - 1))),
                  pl.BlockSpec((D_FF, tno), lambda m, j: (0, jnp.maximum(j - nf, 0)))],
        out_specs=pl.BlockSpec((tm, D_MODEL), lambda m, j: (m, 0)),
        out_shape=jax.ShapeDtypeStruct((t, D_MODEL), F32),
        scratch_shapes=[pltpu.VMEM((tm, D_MODEL), BF16), pltpu.VMEM((nf, tm, tf), BF16)],
        compiler_params=_params(("parallel", "arbitrary")),
        name="ffn",
    )(x1, mod, gain2, gain3, w_ffn_in, w_ffn_in, w_ffn_out)


def _rope_tables(seq):
    pos = jnp.arange(seq, dtype=F32)
    inv_freq = ROPE_BASE ** (-jnp.arange(RET_QK_DIM // 2, dtype=F32) * (2.0 / RET_QK_DIM))
    ang = pos[:, None] * inv_freq[None, :]
    cos, sin = jnp.cos(ang), jnp.sin(ang)
    return jnp.concatenate([cos, cos], axis=-1), jnp.concatenate([-sin, sin], axis=-1)


def _s5_branch(proj, params, batch, seq):
    a_re, a_im, log_dt, b_re, b_im, c_re, c_im, d_skip = params
    t = proj.shape[0]
    rows = t // S5_CHUNK
    e_c, f_c, k_c, lam = _s5_compact(*_s5_prep(a_re, a_im, log_dt, b_re, b_im, c_re, c_im, d_skip))
    u = proj[:, COL_U:COL_U + S5_WIDTH]
    u_im = u.reshape(rows, S5_CHUNK, S5_WIDTH).transpose(1, 0, 2)
    x_prev = _s5_state(u_im, e_c, lam, batch)
    y_im = _s5_out(u_im, x_prev, f_c, k_c)
    return y_im.transpose(1, 0, 2).reshape(t, S5_WIDTH)


def kernel(x, c, w_ada, b_ada, norm_gains, w_in, w_ret_out, ssm_a_re, ssm_a_im, ssm_log_dt,
           ssm_b_re, ssm_b_im, ssm_c_re, ssm_c_im, ssm_d, w_s5_glu, w_out, w_ffn_in, w_ffn_out):
    batch, seq, _ = x.shape
    depth = w_in.shape[0]
    cosf, sinf = _rope_tables(seq)
    xt = x.reshape(batch * seq, D_MODEL)
    for l in range(depth):
        gains = norm_gains[l][:, None, :]
        mod = _adaln_mod(c, w_ada[l], b_ada[l])
        proj = _inproj(xt, mod, gains[0], w_in[l].astype(BF16), cosf, sinf, seq)
        retn = _retention(proj, batch, seq)
        yssm = _s5_branch(proj, (ssm_a_re[l], ssm_a_im[l], ssm_log_dt[l], ssm_b_re[l], ssm_b_im[l],
                                 ssm_c_re[l], ssm_c_im[l], ssm_d[l]), batch, seq)
        merged = _merge(retn, yssm, proj, w_ret_out[l].astype(BF16), w_s5_glu[l].astype(BF16))
        x1 = _oproj(merged, w_out[l].astype(BF16), xt, mod, gains[1], seq)
        xt = _ffn(x1, mod, gains[2], gains[3], w_ffn_in[l].astype(BF16), w_ffn_out[l].astype(BF16), seq)
    return xt.reshape(batch, seq, D_MODEL)
```

```python
import functools
import math

import jax
import jax.numpy as jnp
from jax import lax
from jax.experimental import pallas as pl
from jax.experimental.pallas import tpu as pltpu

F32 = jnp.float32
BF16 = jnp.bfloat16

D_MODEL = 2048
RET_HEADS = 8
RET_QK_DIM = 128
RET_V_DIM = 256
RET_QK_WIDTH = RET_HEADS * RET_QK_DIM
RET_V_WIDTH = RET_HEADS * RET_V_DIM
ROPE_BASE = 10000.0
S5_GROUP = 16
S5_WIDTH = D_MODEL // 2
S5_GROUPS = S5_WIDTH // S5_GROUP
S5_STATE = 64
D_FF = 5632
NORM_EPS = 1e-6
GN_EPS = 1e-5

IN_SIZES = (RET_QK_WIDTH, RET_QK_WIDTH, RET_V_WIDTH, RET_V_WIDTH, S5_WIDTH, D_MODEL, D_MODEL)
IN_WIDTH = sum(IN_SIZES)
COL_Q, COL_K, COL_V, COL_GRET, COL_U, COL_GATE_R, COL_GATE_S = (
    sum(IN_SIZES[:i]) for i in range(len(IN_SIZES)))

V7X_VMEM_BYTES = 64 * 1024 * 1024
VMEM_LIMIT = V7X_VMEM_BYTES - 8 * 1024 * 1024
MXU_COLS = 256
SUBLANES = 8

S5_CHUNK = 16
S5_COL_TILE = 128
S5_GROUPS_PER_TILE = S5_COL_TILE // S5_GROUP
S5_GROUP_STATE = 2 * S5_STATE
S5_STATE_TILE = S5_GROUPS_PER_TILE * S5_GROUP_STATE
S5_TILES = S5_WIDTH // S5_COL_TILE
RET_BLOCK = 256


def _params(semantics):
    return pltpu.CompilerParams(dimension_semantics=semantics, vmem_limit_bytes=VMEM_LIMIT)


def _sigmoid(x):
    return 1.0 / (1.0 + jnp.exp(-x))


def _col_tiles(w, tn):
    k, n = w.shape
    return w.astype(BF16).reshape(k, n // tn, tn).transpose(1, 0, 2)


def _mod_kernel(c_ref, w_ref, b_ref, o_ref):
    @pl.when(pl.program_id(0) == 0)
    def _():
        o_ref[...] = jnp.broadcast_to(b_ref[...], o_ref.shape)

    c = c_ref[...]
    ca = (c * _sigmoid(c)).astype(BF16)
    o_ref[...] += jnp.dot(ca, w_ref[...].astype(BF16), preferred_element_type=F32)


def _adaln_mod(c, w_ada, b_ada):
    bn = c.shape[0]
    rows = 8
    c_pad = jnp.zeros((rows, D_MODEL), F32).at[:bn].set(c)
    tk = 256
    out = pl.pallas_call(
        _mod_kernel,
        grid=(D_MODEL // tk,),
        in_specs=[pl.BlockSpec((rows, tk), lambda k: (0, k)),
                  pl.BlockSpec((tk, 6 * D_MODEL), lambda k: (k, 0)),
                  pl.BlockSpec((1, 6 * D_MODEL), lambda k: (0, 0))],
        out_specs=pl.BlockSpec((rows, 6 * D_MODEL), lambda k: (0, 0)),
        out_shape=jax.ShapeDtypeStruct((rows, 6 * D_MODEL), F32),
        compiler_params=_params(("arbitrary",)),
        name="mod",
    )(c_pad, w_ada, b_ada.reshape(1, -1))
    return out[:bn].reshape(bn, 6, D_MODEL)


def _inproj_kernel(x_ref, mod_ref, g_ref, w_ref, cos_ref, sin_ref, o_ref, h_ref):
    j = pl.program_id(1)

    @pl.when(j == 0)
    def _():
        x = x_ref[...]
        ms = jnp.mean(x * x, axis=-1, keepdims=True)
        y = x * lax.rsqrt(ms + NORM_EPS) * g_ref[...]
        h = y * (1.0 + mod_ref[0, 1:2, :]) + mod_ref[0, 0:1, :]
        h_ref[...] = h.astype(BF16)

    tn = w_ref.shape[1]
    q_tiles = COL_K // tn

    def project(epilogue):
        for k in range(tn // MXU_COLS):
            sl = slice(k * MXU_COLS, (k + 1) * MXU_COLS)
            acc = jnp.dot(h_ref[...], w_ref[:, sl], preferred_element_type=F32)
            o_ref[:, sl] = epilogue(acc).astype(BF16)

    @pl.when(j < COL_V // tn)
    def _():
        scale = jnp.where(j >= q_tiles, RET_QK_DIM ** -0.5, 1.0).astype(F32)
        cosf = cos_ref[...] * scale
        sinf = sin_ref[...] * scale

        def rope(acc):
            heads = []
            for hh in range(MXU_COLS // RET_QK_DIM):
                t = acc[:, hh * RET_QK_DIM:(hh + 1) * RET_QK_DIM]
                heads.append(t * cosf + pltpu.roll(t, RET_QK_DIM // 2, 1) * sinf)
            return jnp.concatenate(heads, axis=1)

        project(rope)

    is_plain = ((j >= COL_V // tn) & (j < COL_GRET // tn)) | ((j >= COL_U // tn) & (j < COL_GATE_R // tn))

    @pl.when(is_plain)
    def _():
        project(lambda acc: acc)

    @pl.when((j >= COL_GRET // tn) & (j < COL_U // tn))
    def _():
        project(lambda acc: acc * _sigmoid(acc))

    @pl.when(j >= COL_GATE_R // tn)
    def _():
        project(_sigmoid)


def _inproj(x2, mod, gain, w_in, cosf, sinf, seq):
    t = x2.shape[0]
    tm, tn = 1024, 1024
    tiles_per_seq = seq // tm
    return pl.pallas_call(
        _inproj_kernel,
        grid=(t // tm, IN_WIDTH // tn),
        in_specs=[pl.BlockSpec((tm, D_MODEL), lambda m, j: (m, 0)),
                  pl.BlockSpec((1, 6, D_MODEL), lambda m, j: (m // tiles_per_seq, 0, 0)),
                  pl.BlockSpec((1, D_MODEL), lambda m, j: (0, 0)),
                  pl.BlockSpec((None, D_MODEL, tn), lambda m, j: (j, 0, 0)),
                  pl.BlockSpec((tm, RET_QK_DIM), lambda m, j: (m % tiles_per_seq, 0)),
                  pl.BlockSpec((tm, RET_QK_DIM), lambda m, j: (m % tiles_per_seq, 0))],
        out_specs=pl.BlockSpec((tm, tn), lambda m, j: (m, j)),
        out_shape=jax.ShapeDtypeStruct((t, IN_WIDTH), BF16),
        scratch_shapes=[pltpu.VMEM((tm, D_MODEL), BF16)],
        compiler_params=_params(("parallel", "arbitrary")),
        name="inproj",
    )(x2, mod, gain, _col_tiles(w_in, tn), cosf, sinf)


def _ret_kernel(q_ref, k_ref, v_ref, g_ref, o_ref, state_ref):
    n = pl.program_id(1)
    cr = q_ref.shape[0]

    @pl.when(n == 0)
    def _():
        state_ref[...] = jnp.zeros_like(state_ref)

    row = lax.broadcasted_iota(jnp.int32, (cr, cr), 0)
    col = lax.broadcasted_iota(jnp.int32, (cr, cr), 1)
    rel = (row - col).astype(F32)
    idx = lax.broadcasted_iota(jnp.int32, (cr, 1), 0).astype(F32)

    for h in range(RET_HEADS):
        log_g = math.log1p(-(2.0 ** (-5.0 - h)))
        decay = jnp.where(rel >= 0.0, jnp.exp(jnp.maximum(rel, 0.0) * log_g), 0.0)
        w_state = jnp.exp((cr - 1.0 - idx) * log_g)
        w_cross = jnp.exp((idx + 1.0) * log_g)
        chunk_decay = math.exp(cr * log_g)

        qs = slice(h * RET_QK_DIM, (h + 1) * RET_QK_DIM)
        vs = slice(h * RET_V_DIM, (h + 1) * RET_V_DIM)
        q = q_ref[:, qs]
        k = k_ref[:, qs]
        v = v_ref[:, vs]
        scores = lax.dot_general(q, k, (((1,), (1,)), ((), ())), preferred_element_type=F32)
        inner = jnp.dot((scores * decay).astype(BF16), v, preferred_element_type=F32)
        state = state_ref[h]
        cross = jnp.dot(q, state.astype(BF16), preferred_element_type=F32) * w_cross
        ret = inner + cross
        mu = jnp.mean(ret, axis=-1, keepdims=True)
        dev = ret - mu
        var = jnp.mean(dev * dev, axis=-1, keepdims=True)
        normed = dev * lax.rsqrt(var + GN_EPS)
        o_ref[:, vs] = (normed * g_ref[:, vs].astype(F32)).astype(BF16)

        kw = (k.astype(F32) * w_state).astype(BF16)
        kv = lax.dot_general(kw, v, (((0,), (0,)), ((), ())), preferred_element_type=F32)
        state_ref[h] = chunk_decay * state + kv


def _retention(proj, batch, seq):
    t = proj.shape[0]
    cr = RET_BLOCK
    nblk = seq // cr
    return pl.pallas_call(
        _ret_kernel,
        grid=(batch, nblk),
        in_specs=[pl.BlockSpec((cr, RET_QK_WIDTH), lambda b, n: (b * nblk + n, COL_Q // RET_QK_WIDTH)),
                  pl.BlockSpec((cr, RET_QK_WIDTH), lambda b, n: (b * nblk + n, COL_K // RET_QK_WIDTH)),
                  pl.BlockSpec((cr, RET_V_WIDTH), lambda b, n: (b * nblk + n, COL_V // RET_V_WIDTH)),
                  pl.BlockSpec((cr, RET_V_WIDTH), lambda b, n: (b * nblk + n, COL_GRET // RET_V_WIDTH))],
        out_specs=pl.BlockSpec((cr, RET_V_WIDTH), lambda b, n: (b * nblk + n, 0)),
        out_shape=jax.ShapeDtypeStruct((t, RET_V_WIDTH), BF16),
        scratch_shapes=[pltpu.VMEM((RET_HEADS, RET_QK_DIM, RET_V_DIM), F32)],
        compiler_params=_params(("parallel", "arbitrary")),
        name="ret",
    )(proj, proj, proj, proj)


def _cmul(ar, ai, br, bi):
    return ar * br - ai * bi, ar * bi + ai * br


def _s5prep_kernel(are_ref, aim_ref, ldt_ref, bre_ref, bim_ref, cre_ref, cim_ref, d_ref,
                   er_ref, ei_ref, fr_ref, fi_ref, k_ref, lr_ref, li_ref):
    ar = are_ref[...]
    ai = aim_ref[...]
    dt = jnp.exp(ldt_ref[...])
    mag = jnp.exp(ar * dt)
    lbr = mag * jnp.cos(ai * dt)
    lbi = mag * jnp.sin(ai * dt)
    nr = lbr - 1.0
    den = ar * ar + ai * ai
    f_re = (nr * ar + lbi * ai) / den
    f_im = (lbi * ar - nr * ai) / den
    bbr, bbi = _cmul(f_re, f_im, bre_ref[...], bim_ref[...])

    rows = S5_CHUNK * S5_GROUP
    gb, _, p = ar.shape
    lag = lax.broadcasted_iota(jnp.int32, (gb, S5_CHUNK, p), 1).astype(F32)

    def lam_pow(k):
        m = jnp.exp(ar * dt * k)
        return m * jnp.cos(ai * dt * k), m * jnp.sin(ai * dt * k)

    def tile_rows(a):
        return jnp.concatenate([a] * S5_CHUNK, axis=1)

    def repeat_rows(a):
        return jnp.concatenate(
            [jnp.broadcast_to(a[:, k:k + 1, :], (gb, S5_GROUP, p)) for k in range(S5_CHUNK)], axis=1)

    p0r, p0i = lam_pow(lag)
    p1r, p1i = _cmul(p0r, p0i, lbr, lbi)
    er, ei = _cmul(repeat_rows(p0r), repeat_rows(p0i), tile_rows(bbr), tile_rows(bbi))
    er_ref[...] = er
    ei_ref[...] = ei
    fr, fi = _cmul(repeat_rows(p1r), repeat_rows(p1i),
                   tile_rows(cre_ref[...]), tile_rows(cim_ref[...]))
    fr_ref[...] = fr
    fi_ref[...] = -fi

    dims = (((2,), (2,)), ((0,), (0,)))
    kern = (lax.dot_general(cre_ref[...], er, dims, preferred_element_type=F32,
                            precision=lax.Precision.HIGHEST)
            - lax.dot_general(cim_ref[...], ei, dims, preferred_element_type=F32,
                              precision=lax.Precision.HIGHEST))
    hrow = lax.broadcasted_iota(jnp.int32, (1, S5_GROUP, rows), 1)
    ncol = lax.broadcasted_iota(jnp.int32, (1, S5_GROUP, rows), 2)
    k_ref[...] = kern + jnp.where(hrow == ncol, d_ref[...], 0.0)

    lcr, lci = lam_pow(float(S5_CHUNK))
    lr_ref[...] = lcr
    li_ref[...] = lci


def _s5_prep(a_re, a_im, log_dt, b_re, b_im, c_re, c_im, d_skip):
    g, p, hg = S5_GROUPS, S5_STATE, S5_GROUP
    gb = 8
    rows = S5_CHUNK * hg
    gp = lambda: pl.BlockSpec((gb, 1, p), lambda i: (i, 0, 0))
    ghp = lambda: pl.BlockSpec((gb, hg, p), lambda i: (i, 0, 0))
    big = lambda: pl.BlockSpec((gb, rows, p), lambda i: (i, 0, 0))
    outs = pl.pallas_call(
        _s5prep_kernel,
        grid=(g // gb,),
        in_specs=[gp(), gp(), gp(), ghp(), ghp(), ghp(), ghp(),
                  pl.BlockSpec((gb, hg, 1), lambda i: (i, 0, 0))],
        out_specs=[big(), big(), big(), big(),
                   pl.BlockSpec((gb, hg, rows), lambda i: (i, 0, 0)), gp(), gp()],
        out_shape=[jax.ShapeDtypeStruct((g, rows, p), F32)] * 4
        + [jax.ShapeDtypeStruct((g, hg, rows), F32)]
        + [jax.ShapeDtypeStruct((g, 1, p), F32)] * 2,
        compiler_params=_params(("arbitrary",)),
        name="s5prep",
    )(a_re[:, None, :], a_im[:, None, :],
      jnp.broadcast_to(log_dt[:, None, None], (g, 1, p)),
      jnp.swapaxes(b_re, 1, 2), jnp.swapaxes(b_im, 1, 2), c_re, c_im, d_skip[:, :, None])
    return outs


def _s5_compact(er, ei, fr, fin, kern, lcr, lci):
    g, hg, p, c = S5_GROUPS, S5_GROUP, S5_STATE, S5_CHUNK
    e = jnp.concatenate([er, ei], axis=-1).reshape(g, c, hg, 2 * p)
    e_c = jnp.flip(e.transpose(1, 0, 2, 3), axis=0).reshape(c, g * hg, 2 * p).astype(BF16)
    f = jnp.concatenate([fr, fin], axis=-1).reshape(g, c, hg, 2 * p)
    f_c = f.transpose(1, 3, 0, 2).reshape(c, 2 * p, g * hg).astype(BF16)
    k_c = kern.reshape(g, hg, c, hg).transpose(2, 3, 0, 1).reshape(c, hg, g * hg).astype(BF16)
    lam = jnp.stack([jnp.concatenate([lcr, lcr], axis=-1).reshape(S5_TILES, S5_STATE_TILE),
                     jnp.concatenate([lci, lci], axis=-1).reshape(S5_TILES, S5_STATE_TILE)], axis=1)
    return e_c, f_c, k_c, lam


def _s5state_kernel(u_ref, e_ref, lam_ref, o_ref, w_ref, acc_ref, swp_ref, *, batch, nchunk):
    c = u_ref.shape[0]
    ct, gs, hg = S5_COL_TILE, S5_GROUP_STATE, S5_GROUP

    @pl.when(pl.program_id(0) == 0)
    def _():
        w_ref[...] = jnp.zeros_like(w_ref)

    for i in range(c):
        for gl in range(S5_GROUPS_PER_TILE):
            w_ref[i * ct + gl * hg:i * ct + (gl + 1) * hg, gl * gs:(gl + 1) * gs] = (
                e_ref[i, gl * hg:(gl + 1) * hg, :])

    lhs = jnp.concatenate([u_ref[i] for i in range(c)], axis=1)
    s = jnp.dot(lhs, w_ref[...], preferred_element_type=F32)
    width = s.shape[1]
    is_re = (lax.broadcasted_iota(jnp.int32, (1, width), 1) % gs) < S5_STATE
    acc_ref[...] = s
    swp_ref[...] = jnp.where(is_re, pltpu.roll(s, width - S5_STATE, 1), pltpu.roll(s, S5_STATE, 1))

    lr = lam_ref[0, 0:1, :]
    li = lam_ref[0, 1:2, :]
    lx = jnp.where(is_re, -li, li)

    sub = SUBLANES // batch
    first = lax.broadcasted_iota(jnp.int32, (SUBLANES, width), 0) < batch

    def step(m, carry):
        c, cs = carry
        r0 = pl.multiple_of(m * SUBLANES, SUBLANES)
        v = acc_ref[pl.ds(r0, SUBLANES), :]
        vs = swp_ref[pl.ds(r0, SUBLANES), :]
        t = lr * c + lx * cs + v
        ts = lr * cs - lx * c + vs
        t_sh = pltpu.roll(t, batch, 0)
        ts_sh = pltpu.roll(ts, batch, 0)
        t2 = lr * t_sh + lx * ts_sh + v
        t2s = lr * ts_sh - lx * t_sh + vs
        acc_ref[pl.ds(r0, SUBLANES), :] = jnp.where(first, c, t_sh)
        return (jnp.where(first, pltpu.roll(t2, batch, 0), t2),
                jnp.where(first, pltpu.roll(t2s, batch, 0), t2s))

    assert sub == 2 and (batch * nchunk) % SUBLANES == 0
    zero = jnp.zeros((SUBLANES, width), F32)
    lax.fori_loop(0, batch * nchunk // SUBLANES, step, (zero, zero))
    o_ref[...] = acc_ref[...].astype(BF16)


def _s5_state(u_im, e_c, lam, batch):
    c, rows, _ = u_im.shape
    return pl.pallas_call(
        functools.partial(_s5state_kernel, batch=batch, nchunk=rows // batch),
        grid=(S5_TILES,),
        in_specs=[pl.BlockSpec((c, rows, S5_COL_TILE), lambda t: (0, 0, t)),
                  pl.BlockSpec((c, S5_COL_TILE, S5_GROUP_STATE), lambda t: (0, t, 0)),
                  pl.BlockSpec((1, 2, S5_STATE_TILE), lambda t: (t, 0, 0))],
        out_specs=pl.BlockSpec((rows, S5_STATE_TILE), lambda t: (0, t)),
        out_shape=jax.ShapeDtypeStruct((rows, S5_TILES * S5_STATE_TILE), BF16),
        scratch_shapes=[pltpu.VMEM((c * S5_COL_TILE, S5_STATE_TILE), BF16),
                        pltpu.VMEM((rows, S5_STATE_TILE), F32),
                        pltpu.VMEM((rows, S5_STATE_TILE), F32)],
        compiler_params=_params(("arbitrary",)),
        name="s5state",
    )(u_im, e_c, lam)


def _s5out_kernel(u_ref, x_ref, f_ref, k_ref, o_ref, w_ref, lhs_ref):
    c = u_ref.shape[0]
    ct, gs, hg, st = S5_COL_TILE, S5_GROUP_STATE, S5_GROUP, S5_STATE_TILE

    @pl.when(pl.program_id(0) == 0)
    def _():
        w_ref[...] = jnp.zeros_like(w_ref)

    lane_group = lax.broadcasted_iota(jnp.int32, (gs, ct), 1) // hg
    for i in range(c):
        f_i = f_ref[i]
        for gl in range(S5_GROUPS_PER_TILE):
            w_ref[gl * gs:(gl + 1) * gs, i * ct:(i + 1) * ct] = jnp.where(
                lane_group == gl, f_i, jnp.zeros_like(f_i))
    lane_group = lax.broadcasted_iota(jnp.int32, (hg, ct), 1) // hg
    for j in range(c):
        k_j = k_ref[j]
        lag_block = jnp.concatenate(
            [jnp.where(lane_group == gl, k_j, jnp.zeros_like(k_j)) for gl in range(S5_GROUPS_PER_TILE)],
            axis=0)
        for i0 in range(c - j):
            w_ref[st + i0 * ct:st + (i0 + 1) * ct, (i0 + j) * ct:(i0 + j + 1) * ct] = lag_block

    lhs_ref[:, 0:st] = x_ref[...]
    for i in range(c):
        lhs_ref[:, st + i * ct:st + (i + 1) * ct] = u_ref[i]

    for n in range(c // 2):
        k = st + 2 * ct * (n + 1)
        acc = jnp.dot(lhs_ref[:, 0:k], w_ref[0:k, 2 * n * ct:2 * (n + 1) * ct], preferred_element_type=F32)
        y = jax.nn.gelu(acc).astype(BF16)
        o_ref[2 * n] = y[:, 0:ct]
        o_ref[2 * n + 1] = y[:, ct:2 * ct]


def _s5_out(u_im, x_prev, f_c, k_c):
    c, rows, width = u_im.shape
    return pl.pallas_call(
        _s5out_kernel,
        grid=(S5_TILES,),
        in_specs=[pl.BlockSpec((c, rows, S5_COL_TILE), lambda t: (0, 0, t)),
                  pl.BlockSpec((rows, S5_STATE_TILE), lambda t: (0, t)),
                  pl.BlockSpec((c, S5_GROUP_STATE, S5_COL_TILE), lambda t: (0, 0, t)),
                  pl.BlockSpec((c, S5_GROUP, S5_COL_TILE), lambda t: (0, 0, t))],
        out_specs=pl.BlockSpec((c, rows, S5_COL_TILE), lambda t: (0, 0, t)),
        out_shape=jax.ShapeDtypeStruct((c, rows, width), BF16),
        scratch_shapes=[pltpu.VMEM((S5_STATE_TILE + c * S5_COL_TILE, c * S5_COL_TILE), BF16),
                        pltpu.VMEM((rows, S5_STATE_TILE + c * S5_COL_TILE), BF16)],
        compiler_params=_params(("arbitrary",)),
        name="s5out",
    )(u_im, x_prev, f_c, k_c)


def _merge_kernel(ret_ref, ssm_ref, wro_ref, wga_ref, wgb_ref, gr_ref, gs_ref, o_ref):
    for k in range(o_ref.shape[1] // MXU_COLS):
        sl = slice(k * MXU_COLS, (k + 1) * MXU_COLS)
        y_ret = jnp.dot(ret_ref[...], wro_ref[:, sl], preferred_element_type=F32)
        glu_a = jnp.dot(ssm_ref[...], wga_ref[:, sl], preferred_element_type=F32)
        glu_b = jnp.dot(ssm_ref[...], wgb_ref[:, sl], preferred_element_type=F32)
        y_s5 = glu_a * _sigmoid(glu_b)
        merged = gr_ref[:, sl].astype(F32) * y_ret + gs_ref[:, sl].astype(F32) * y_s5
        o_ref[:, sl] = merged.astype(BF16)


def _merge(retn, yssm, proj, w_ret_out, w_s5_glu):
    t = retn.shape[0]
    tm, tn = 1024, 512
    nb = D_MODEL // tn
    glu_tiles = _col_tiles(w_s5_glu, tn)
    return pl.pallas_call(
        _merge_kernel,
        grid=(t // tm, nb),
        in_specs=[pl.BlockSpec((tm, RET_V_WIDTH), lambda m, j: (m, 0)),
                  pl.BlockSpec((tm, S5_WIDTH), lambda m, j: (m, 0)),
                  pl.BlockSpec((None, RET_V_WIDTH, tn), lambda m, j: (j, 0, 0)),
                  pl.BlockSpec((None, S5_WIDTH, tn), lambda m, j: (j, 0, 0)),
                  pl.BlockSpec((None, S5_WIDTH, tn), lambda m, j: (nb + j, 0, 0)),
                  pl.BlockSpec((tm, tn), lambda m, j: (m, COL_GATE_R // tn + j)),
                  pl.BlockSpec((tm, tn), lambda m, j: (m, COL_GATE_S // tn + j))],
        out_specs=pl.BlockSpec((tm, tn), lambda m, j: (m, j)),
        out_shape=jax.ShapeDtypeStruct((t, D_MODEL), BF16),
        compiler_params=_params(("parallel", "arbitrary")),
        name="merge",
    )(retn, yssm, _col_tiles(w_ret_out, tn), glu_tiles, glu_tiles, proj, proj)


def _oproj_kernel(a_ref, w_ref, x_ref, mod_ref, g_ref, o_ref):
    y = jnp.dot(a_ref[...], w_ref[...], preferred_element_type=F32)
    ms = jnp.mean(y * y, axis=-1, keepdims=True)
    yn = y * lax.rsqrt(ms + NORM_EPS) * g_ref[...]
    o_ref[...] = x_ref[...] + mod_ref[0, 2:3, :] * yn


def _oproj(merged, w_out, x2, mod, gain, seq):
    t = merged.shape[0]
    tm = 512
    tiles_per_seq = seq // tm
    return pl.pallas_call(
        _oproj_kernel,
        grid=(t // tm,),
        in_specs=[pl.BlockSpec((tm, D_MODEL), lambda m: (m, 0)),
                  pl.BlockSpec((D_MODEL, D_MODEL), lambda m: (0, 0)),
                  pl.BlockSpec((tm, D_MODEL), lambda m: (m, 0)),
                  pl.BlockSpec((1, 6, D_MODEL), lambda m: (m // tiles_per_seq, 0, 0)),
                  pl.BlockSpec((1, D_MODEL), lambda m: (0, 0))],
        out_specs=pl.BlockSpec((tm, D_MODEL), lambda m: (m, 0)),
        out_shape=jax.ShapeDtypeStruct((t, D_MODEL), F32),
        compiler_params=_params(("parallel",)),
        name="oproj",
    )(merged, w_out, x2, mod, gain)


def _ffn_kernel(x_ref, mod_ref, g2_ref, g3_ref, wa_ref, wb_ref, wo_ref, o_ref, h_ref, act_ref, *, nf, no):
    j = pl.program_id(1)
    tf = wa_ref.shape[1]
    tno = wo_ref.shape[1]

    @pl.when(j == 0)
    def _():
        x = x_ref[...]
        ms = jnp.mean(x * x, axis=-1, keepdims=True)
        y = x * lax.rsqrt(ms + NORM_EPS) * g2_ref[...]
        h_ref[...] = (y * (1.0 + mod_ref[0, 4:5, :]) + mod_ref[0, 3:4, :]).astype(BF16)

    @pl.when(j < nf)
    def _():
        parts = []
        for k in range(tf // MXU_COLS):
            sl = slice(k * MXU_COLS, (k + 1) * MXU_COLS)
            a = jnp.dot(h_ref[...], wa_ref[:, sl], preferred_element_type=F32)
            b = jnp.dot(h_ref[...], wb_ref[:, sl], preferred_element_type=F32)
            parts.append((a * _sigmoid(a) * b).astype(BF16))
        act_ref[j] = jnp.concatenate(parts, axis=1)

    for n in range(no):
        @pl.when(j == nf + n)
        def _(n=n):
            y = jnp.dot(act_ref[0], wo_ref[0:tf, :], preferred_element_type=F32)
            for jj in range(1, nf):
                y += jnp.dot(act_ref[jj], wo_ref[jj * tf:(jj + 1) * tf, :], preferred_element_type=F32)
            o_ref[:, n * tno:(n + 1) * tno] = y

    @pl.when(j == nf + no - 1)
    def _():
        y = o_ref[...]
        ms = jnp.mean(y * y, axis=-1, keepdims=True)
        yn = y * lax.rsqrt(ms + NORM_EPS) * g3_ref[...]
        o_ref[...] = x_ref[...] + mod_ref[0, 5:6, :] * yn


def _ffn(x1, mod, gain2, gain3, w_ffn_in, w_ffn_out, seq):
    t = x1.shape[0]
    tm, tf, tno = 512, 512, 512
    nf = D_FF // tf
    no = D_MODEL // tno
    tiles_per_seq = seq // tm
    in_tiles = _col_tiles(w_ffn_in, tf)
    return pl.pallas_call(
        functools.partial(_ffn_kernel, nf=nf, no=no),
        grid=(t // tm, nf + no),
        in_specs=[pl.BlockSpec((tm, D_MODEL), lambda m, j: (m, 0)),
                  pl.BlockSpec((1, 6, D_MODEL), lambda m, j: (m // tiles_per_seq, 0, 0)),
                  pl.BlockSpec((1, D_MODEL), lambda m, j: (0, 0)),
                  pl.BlockSpec((1, D_MODEL), lambda m, j: (0, 0)),
                  pl.BlockSpec((None, D_MODEL, tf), lambda m, j: (jnp.minimum(j, nf - 1), 0, 0)),
                  pl.BlockSpec((None, D_MODEL, tf), lambda m, j: (nf + jnp.minimum(j, nf - 1), 0, 0)),
                  pl.BlockSpec((None, D_FF, tno), lambda m, j: (jnp.maximum(j - nf, 0), 0, 0))],
        out_specs=pl.BlockSpec((tm, D_MODEL), lambda m, j: (m, 0)),
        out_shape=jax.ShapeDtypeStruct((t, D_MODEL), F32),
        scratch_shapes=[pltpu.VMEM((tm, D_MODEL), BF16), pltpu.VMEM((nf, tm, tf), BF16)],
        compiler_params=_params(("parallel", "arbitrary")),
        name="ffn",
    )(x1, mod, gain2, gain3, in_tiles, in_tiles, _col_tiles(w_ffn_out, tno))


def _rope_tables(seq):
    pos = jnp.arange(seq, dtype=F32)
    inv_freq = ROPE_BASE ** (-jnp.arange(RET_QK_DIM // 2, dtype=F32) * (2.0 / RET_QK_DIM))
    ang = pos[:, None] * inv_freq[None, :]
    cos, sin = jnp.cos(ang), jnp.sin(ang)
    return jnp.concatenate([cos, cos], axis=-1), jnp.concatenate([-sin, sin], axis=-1)


def _s5_branch(proj, params, batch, seq):
    a_re, a_im, log_dt, b_re, b_im, c_re, c_im, d_skip = params
    t = proj.shape[0]
    rows = t // S5_CHUNK
    e_c, f_c, k_c, lam = _s5_compact(*_s5_prep(a_re, a_im, log_dt, b_re, b_im, c_re, c_im, d_skip))
    nchunk = seq // S5_CHUNK
    u = proj[:, COL_U:COL_U + S5_WIDTH].reshape(batch, nchunk, S5_CHUNK, S5_WIDTH)
    u_im = u.transpose(2, 1, 0, 3).reshape(S5_CHUNK, rows, S5_WIDTH)
    x_prev = _s5_state(u_im, e_c, lam, batch)
    y_im = _s5_out(u_im, x_prev, f_c, k_c).reshape(S5_CHUNK, nchunk, batch, S5_WIDTH)
    return y_im.transpose(2, 1, 0, 3).reshape(t, S5_WIDTH)


def kernel(x, c, w_ada, b_ada, norm_gains, w_in, w_ret_out, ssm_a_re, ssm_a_im, ssm_log_dt,
           ssm_b_re, ssm_b_im, ssm_c_re, ssm_c_im, ssm_d, w_s5_glu, w_out, w_ffn_in, w_ffn_out):
    batch, seq, _ = x.shape
    depth = w_in.shape[0]
    cosf, sinf = _rope_tables(seq)
    xt = x.reshape(batch * seq, D_MODEL)
    for l in range(depth):
        gains = norm_gains[l][:, None, :]
        mod = _adaln_mod(c, w_ada[l], b_ada[l])
        proj = _inproj(xt, mod, gains[0], w_in[l], cosf, sinf, seq)
        retn = _retention(proj, batch, seq)
        yssm = _s5_branch(proj, (ssm_a_re[l], ssm_a_im[l], ssm_log_dt[l], ssm_b_re[l], ssm_b_im[l],
                                 ssm_c_re[l], ssm_c_im[l], ssm_d[l]), batch, seq)
        merged = _merge(retn, yssm, proj, w_ret_out[l], w_s5_glu[l])
        x1 = _oproj(merged, w_out[l].astype(BF16), xt, mod, gains[1], seq)
        xt = _ffn(x1, mod, gains[2], gains[3], w_ffn_in[l], w_ffn_out[l], seq)
    return xt.reshape(batch, seq, D_MODEL)
```

```python
import functools
import math

import jax
import jax.numpy as jnp
from jax import lax
from jax.experimental import pallas as pl
from jax.experimental.pallas import tpu as pltpu

F32 = jnp.float32
BF16 = jnp.bfloat16

D_MODEL = 2048
RET_HEADS = 8
RET_QK_DIM = 128
RET_V_DIM = 256
RET_QK_WIDTH = RET_HEADS * RET_QK_DIM
RET_V_WIDTH = RET_HEADS * RET_V_DIM
ROPE_BASE = 10000.0
S5_GROUP = 16
S5_WIDTH = D_MODEL // 2
S5_GROUPS = S5_WIDTH // S5_GROUP
S5_STATE = 64
D_FF = 5632
NORM_EPS = 1e-6
GN_EPS = 1e-5

IN_SIZES = (RET_QK_WIDTH, RET_QK_WIDTH, RET_V_WIDTH, RET_V_WIDTH, S5_WIDTH, D_MODEL, D_MODEL)
IN_WIDTH = sum(IN_SIZES)
COL_Q, COL_K, COL_V, COL_GRET, COL_U, COL_GATE_R, COL_GATE_S = (
    sum(IN_SIZES[:i]) for i in range(len(IN_SIZES)))
PROJ_SIZES = IN_SIZES[:4] + IN_SIZES[5:]
PROJ_WIDTH = sum(PROJ_SIZES)
PROJ_Q, PROJ_K, PROJ_V, PROJ_GRET, PROJ_GATE_R, PROJ_GATE_S = (
    sum(PROJ_SIZES[:i]) for i in range(len(PROJ_SIZES)))

V7X_VMEM_BYTES = 64 * 1024 * 1024
VMEM_LIMIT = V7X_VMEM_BYTES - 8 * 1024 * 1024
MXU_COLS = 256
SUBLANES = 8

S5_CHUNK = 16
S5_COL_TILE = 128
S5_GROUPS_PER_TILE = S5_COL_TILE // S5_GROUP
S5_GROUP_STATE = 2 * S5_STATE
S5_STATE_TILE = S5_GROUPS_PER_TILE * S5_GROUP_STATE
S5_TILES = S5_WIDTH // S5_COL_TILE
RET_BLOCK = 256


def _params(semantics):
    return pltpu.CompilerParams(dimension_semantics=semantics, vmem_limit_bytes=VMEM_LIMIT)


def _sigmoid(x):
    return 0.5 * jnp.tanh(0.5 * x) + 0.5


def _silu(x):
    half = 0.5 * x
    return half + half * jnp.tanh(half)


def _mod_kernel(c_ref, w_ref, b_ref, o_ref):
    @pl.when(pl.program_id(0) == 0)
    def _():
        o_ref[...] = jnp.broadcast_to(b_ref[...], o_ref.shape)

    c = c_ref[...]
    ca = (c * _sigmoid(c)).astype(BF16)
    o_ref[...] += jnp.dot(ca, w_ref[...].astype(BF16), preferred_element_type=F32)


def _adaln_mod(c, w_ada, b_ada):
    bn = c.shape[0]
    rows = 8
    c_pad = jnp.zeros((rows, D_MODEL), F32).at[:bn].set(c)
    tk = 256
    out = pl.pallas_call(
        _mod_kernel,
        grid=(D_MODEL // tk,),
        in_specs=[pl.BlockSpec((rows, tk), lambda k: (0, k)),
                  pl.BlockSpec((tk, 6 * D_MODEL), lambda k: (k, 0)),
                  pl.BlockSpec((1, 6 * D_MODEL), lambda k: (0, 0))],
        out_specs=pl.BlockSpec((rows, 6 * D_MODEL), lambda k: (0, 0)),
        out_shape=jax.ShapeDtypeStruct((rows, 6 * D_MODEL), F32),
        compiler_params=_params(("arbitrary",)),
        name="mod",
    )(c_pad, w_ada, b_ada.reshape(1, -1))
    return out[:bn].reshape(bn, 6, D_MODEL)


def _inproj_kernel(x_ref, mod_ref, g_ref, w_ref, cos_ref, sin_ref, o_ref, u_ref, h_ref):
    j = pl.program_id(1)

    @pl.when(j == 0)
    def _():
        x = x_ref[...]
        ms = jnp.mean(x * x, axis=-1, keepdims=True)
        y = x * lax.rsqrt(ms + NORM_EPS) * g_ref[...]
        h = y * (1.0 + mod_ref[0, 1:2, :]) + mod_ref[0, 0:1, :]
        h_ref[...] = h.astype(BF16)

    tn = w_ref.shape[1]
    q_tiles = COL_K // tn

    def project(epilogue, out_ref):
        for k in range(tn // MXU_COLS):
            sl = slice(k * MXU_COLS, (k + 1) * MXU_COLS)
            acc = jnp.dot(h_ref[...], w_ref[:, sl], preferred_element_type=F32)
            out_ref[:, sl] = epilogue(acc).astype(BF16)

    @pl.when(j < COL_V // tn)
    def _():
        scale = jnp.where(j >= q_tiles, RET_QK_DIM ** -0.5, 1.0).astype(F32)
        cosf = cos_ref[...] * scale
        sinf = sin_ref[...] * scale

        def rope(acc):
            heads = []
            for hh in range(MXU_COLS // RET_QK_DIM):
                t = acc[:, hh * RET_QK_DIM:(hh + 1) * RET_QK_DIM]
                heads.append(t * cosf + pltpu.roll(t, RET_QK_DIM // 2, 1) * sinf)
            return jnp.concatenate(heads, axis=1)

        project(rope, o_ref)

    @pl.when((j >= COL_V // tn) & (j < COL_GRET // tn))
    def _():
        project(lambda acc: acc, o_ref)

    @pl.when((j >= COL_GRET // tn) & (j < COL_U // tn))
    def _():
        project(_silu, o_ref)

    @pl.when((j >= COL_U // tn) & (j < COL_GATE_R // tn))
    def _():
        project(lambda acc: acc, u_ref)

    @pl.when(j >= COL_GATE_R // tn)
    def _():
        project(_sigmoid, o_ref)


def _inproj(x2, mod, gain, w_in, cosf, sinf, seq):
    t = x2.shape[0]
    tm, tn = 1024, 1024
    assert S5_WIDTH == tn
    u_tile = COL_U // tn
    tiles_per_seq = seq // tm
    return pl.pallas_call(
        _inproj_kernel,
        grid=(t // tm, IN_WIDTH // tn),
        in_specs=[pl.BlockSpec((tm, D_MODEL), lambda m, j: (m, 0)),
                  pl.BlockSpec((1, 6, D_MODEL), lambda m, j: (m // tiles_per_seq, 0, 0)),
                  pl.BlockSpec((1, D_MODEL), lambda m, j: (0, 0)),
                  pl.BlockSpec((D_MODEL, tn), lambda m, j: (0, j)),
                  pl.BlockSpec((tm, RET_QK_DIM), lambda m, j: (m % tiles_per_seq, 0)),
                  pl.BlockSpec((tm, RET_QK_DIM), lambda m, j: (m % tiles_per_seq, 0))],
        out_specs=[pl.BlockSpec((tm, tn), lambda m, j: (m, jnp.where(j >= u_tile, j - 1, j))),
                   pl.BlockSpec((tm, tn), lambda m, j: (m, 0))],
        out_shape=[jax.ShapeDtypeStruct((t, PROJ_WIDTH), BF16),
                   jax.ShapeDtypeStruct((t, S5_WIDTH), BF16)],
        scratch_shapes=[pltpu.VMEM((tm, D_MODEL), BF16)],
        compiler_params=_params(("parallel", "arbitrary")),
        name="inproj",
    )(x2, mod, gain, w_in, cosf, sinf)


def _ret_kernel(q_ref, k_ref, v_ref, g_ref, o_ref, state_ref, decay_ref, wcross_ref, wstate_ref):
    n = pl.program_id(1)
    cr = q_ref.shape[0]

    @pl.when(n == 0)
    def _():
        state_ref[...] = jnp.zeros_like(state_ref)

    @pl.when((pl.program_id(0) == 0) & (n == 0))
    def _():
        row = lax.broadcasted_iota(jnp.int32, (cr, cr), 0)
        col = lax.broadcasted_iota(jnp.int32, (cr, cr), 1)
        rel = (row - col).astype(F32)
        idx_v = lax.broadcasted_iota(jnp.int32, (cr, RET_V_DIM), 0).astype(F32)
        idx_k = lax.broadcasted_iota(jnp.int32, (cr, RET_QK_DIM), 0).astype(F32)
        for h in range(RET_HEADS):
            log_g = math.log1p(-(2.0 ** (-5.0 - h)))
            decay_ref[h] = jnp.where(rel >= 0.0, jnp.exp(jnp.maximum(rel, 0.0) * log_g), 0.0)
            wcross_ref[h] = jnp.exp((idx_v + 1.0) * log_g)
            wstate_ref[h] = jnp.exp((cr - 1.0 - idx_k) * log_g)

    for h in range(RET_HEADS):
        chunk_decay = math.exp(cr * math.log1p(-(2.0 ** (-5.0 - h))))
        qs = slice(h * RET_QK_DIM, (h + 1) * RET_QK_DIM)
        vs = slice(h * RET_V_DIM, (h + 1) * RET_V_DIM)
        q = q_ref[:, qs]
        k = k_ref[:, qs]
        v = v_ref[:, vs]
        scores = lax.dot_general(q, k, (((1,), (1,)), ((), ())), preferred_element_type=F32)
        inner = jnp.dot((scores * decay_ref[h]).astype(BF16), v, preferred_element_type=F32)
        state = state_ref[h]
        cross = jnp.dot(q, state.astype(BF16), preferred_element_type=F32) * wcross_ref[h]
        ret = inner + cross
        mu = jnp.mean(ret, axis=-1, keepdims=True)
        dev = ret - mu
        var = jnp.mean(dev * dev, axis=-1, keepdims=True)
        normed = dev * lax.rsqrt(var + GN_EPS)
        o_ref[:, vs] = (normed * g_ref[:, vs].astype(F32)).astype(BF16)

        kw = (k.astype(F32) * wstate_ref[h]).astype(BF16)
        kv = lax.dot_general(kw, v, (((0,), (0,)), ((), ())), preferred_element_type=F32)
        state_ref[h] = chunk_decay * state + kv


def _retention(proj, batch, seq):
    t = proj.shape[0]
    cr = RET_BLOCK
    nblk = seq // cr
    return pl.pallas_call(
        _ret_kernel,
        grid=(batch, nblk),
        in_specs=[pl.BlockSpec((cr, RET_QK_WIDTH), lambda b, n: (b * nblk + n, PROJ_Q // RET_QK_WIDTH)),
                  pl.BlockSpec((cr, RET_QK_WIDTH), lambda b, n: (b * nblk + n, PROJ_K // RET_QK_WIDTH)),
                  pl.BlockSpec((cr, RET_V_WIDTH), lambda b, n: (b * nblk + n, PROJ_V // RET_V_WIDTH)),
                  pl.BlockSpec((cr, RET_V_WIDTH), lambda b, n: (b * nblk + n, PROJ_GRET // RET_V_WIDTH))],
        out_specs=pl.BlockSpec((cr, RET_V_WIDTH), lambda b, n: (b * nblk + n, 0)),
        out_shape=jax.ShapeDtypeStruct((t, RET_V_WIDTH), BF16),
        scratch_shapes=[pltpu.VMEM((RET_HEADS, RET_QK_DIM, RET_V_DIM), F32),
                        pltpu.VMEM((RET_HEADS, cr, cr), F32),
                        pltpu.VMEM((RET_HEADS, cr, RET_V_DIM), F32),
                        pltpu.VMEM((RET_HEADS, cr, RET_QK_DIM), F32)],
        compiler_params=_params(("arbitrary", "arbitrary")),
        name="ret",
    )(proj, proj, proj, proj)


def _cmul(ar, ai, br, bi):
    return ar * br - ai * bi, ar * bi + ai * br


def _s5prep_kernel(are_ref, aim_ref, ldt_ref, bre_ref, bim_ref, cre_ref, cim_ref, d_ref,
                   er_ref, ei_ref, fr_ref, fi_ref, k_ref, lr_ref, li_ref):
    ar = are_ref[...]
    ai = aim_ref[...]
    dt = jnp.exp(ldt_ref[...])
    mag = jnp.exp(ar * dt)
    lbr = mag * jnp.cos(ai * dt)
    lbi = mag * jnp.sin(ai * dt)
    nr = lbr - 1.0
    den = ar * ar + ai * ai
    f_re = (nr * ar + lbi * ai) / den
    f_im = (lbi * ar - nr * ai) / den
    bbr, bbi = _cmul(f_re, f_im, bre_ref[...], bim_ref[...])

    rows = S5_CHUNK * S5_GROUP
    gb, _, p = ar.shape
    lag = lax.broadcasted_iota(jnp.int32, (gb, S5_CHUNK, p), 1).astype(F32)

    def lam_pow(k):
        m = jnp.exp(ar * dt * k)
        return m * jnp.cos(ai * dt * k), m * jnp.sin(ai * dt * k)

    def tile_rows(a):
        return jnp.concatenate([a] * S5_CHUNK, axis=1)

    def repeat_rows(a):
        return jnp.concatenate(
            [jnp.broadcast_to(a[:, k:k + 1, :], (gb, S5_GROUP, p)) for k in range(S5_CHUNK)], axis=1)

    p0r, p0i = lam_pow(lag)
    p1r, p1i = _cmul(p0r, p0i, lbr, lbi)
    er, ei = _cmul(repeat_rows(p0r), repeat_rows(p0i), tile_rows(bbr), tile_rows(bbi))
    er_ref[...] = er
    ei_ref[...] = ei
    fr, fi = _cmul(repeat_rows(p1r), repeat_rows(p1i),
                   tile_rows(cre_ref[...]), tile_rows(cim_ref[...]))
    fr_ref[...] = fr
    fi_ref[...] = -fi

    dims = (((2,), (2,)), ((0,), (0,)))
    kern = (lax.dot_general(cre_ref[...], er, dims, preferred_element_type=F32,
                            precision=lax.Precision.HIGHEST)
            - lax.dot_general(cim_ref[...], ei, dims, preferred_element_type=F32,
                              precision=lax.Precision.HIGHEST))
    hrow = lax.broadcasted_iota(jnp.int32, (1, S5_GROUP, rows), 1)
    ncol = lax.broadcasted_iota(jnp.int32, (1, S5_GROUP, rows), 2)
    k_ref[...] = kern + jnp.where(hrow == ncol, d_ref[...], 0.0)

    lcr, lci = lam_pow(float(S5_CHUNK))
    lr_ref[...] = lcr
    li_ref[...] = lci


def _s5_prep(a_re, a_im, log_dt, b_re, b_im, c_re, c_im, d_skip):
    g, p, hg = S5_GROUPS, S5_STATE, S5_GROUP
    gb = 8
    rows = S5_CHUNK * hg
    gp = lambda: pl.BlockSpec((gb, 1, p), lambda i: (i, 0, 0))
    ghp = lambda: pl.BlockSpec((gb, hg, p), lambda i: (i, 0, 0))
    big = lambda: pl.BlockSpec((gb, rows, p), lambda i: (i, 0, 0))
    outs = pl.pallas_call(
        _s5prep_kernel,
        grid=(g // gb,),
        in_specs=[gp(), gp(), gp(), ghp(), ghp(), ghp(), ghp(),
                  pl.BlockSpec((gb, hg, 1), lambda i: (i, 0, 0))],
        out_specs=[big(), big(), big(), big(),
                   pl.BlockSpec((gb, hg, rows), lambda i: (i, 0, 0)), gp(), gp()],
        out_shape=[jax.ShapeDtypeStruct((g, rows, p), F32)] * 4
        + [jax.ShapeDtypeStruct((g, hg, rows), F32)]
        + [jax.ShapeDtypeStruct((g, 1, p), F32)] * 2,
        compiler_params=_params(("arbitrary",)),
        name="s5prep",
    )(a_re[:, None, :], a_im[:, None, :],
      jnp.broadcast_to(log_dt[:, None, None], (g, 1, p)),
      jnp.swapaxes(b_re, 1, 2), jnp.swapaxes(b_im, 1, 2), c_re, c_im, d_skip[:, :, None])
    return outs


def _s5_compact(er, ei, fr, fin, kern, lcr, lci):
    g, hg, p, c = S5_GROUPS, S5_GROUP, S5_STATE, S5_CHUNK
    e = jnp.concatenate([er, ei], axis=-1).reshape(g, c, hg, 2 * p)
    e_c = jnp.flip(e.transpose(1, 0, 2, 3), axis=0).reshape(c, g * hg, 2 * p).astype(BF16)
    f = jnp.concatenate([fr, fin], axis=-1).reshape(g, c, hg, 2 * p)
    f_c = f.transpose(1, 3, 0, 2).reshape(c, 2 * p, g * hg).astype(BF16)
    k_c = kern.reshape(g, hg, c, hg).transpose(2, 3, 0, 1).reshape(c, hg, g * hg).astype(BF16)
    lam = jnp.stack([jnp.concatenate([lcr, lcr], axis=-1).reshape(S5_TILES, S5_STATE_TILE),
                     jnp.concatenate([lci, lci], axis=-1).reshape(S5_TILES, S5_STATE_TILE)], axis=1)
    return e_c, f_c, k_c, lam


def _s5state_kernel(u_ref, e_ref, lam_ref, o_ref, w_ref, acc_ref, swp_ref, *, batch, nchunk):
    c = u_ref.shape[0]
    ct, gs, hg = S5_COL_TILE, S5_GROUP_STATE, S5_GROUP

    @pl.when(pl.program_id(0) == 0)
    def _():
        w_ref[...] = jnp.zeros_like(w_ref)

    for i in range(c):
        for gl in range(S5_GROUPS_PER_TILE):
            w_ref[i * ct + gl * hg:i * ct + (gl + 1) * hg, gl * gs:(gl + 1) * gs] = (
                e_ref[i, gl * hg:(gl + 1) * hg, :])

    lhs = jnp.concatenate([u_ref[i] for i in range(c)], axis=1)
    s = jnp.dot(lhs, w_ref[...], preferred_element_type=F32)
    width = s.shape[1]
    is_re = (lax.broadcasted_iota(jnp.int32, (1, width), 1) % gs) < S5_STATE
    acc_ref[...] = s
    swp_ref[...] = jnp.where(is_re, pltpu.roll(s, width - S5_STATE, 1), pltpu.roll(s, S5_STATE, 1))

    lr = lam_ref[0, 0:1, :]
    li = lam_ref[0, 1:2, :]
    lx = jnp.where(is_re, -li, li)

    sub = SUBLANES // batch
    first = lax.broadcasted_iota(jnp.int32, (SUBLANES, width), 0) < batch

    def step(m, carry):
        c, cs = carry
        r0 = pl.multiple_of(m * SUBLANES, SUBLANES)
        v = acc_ref[pl.ds(r0, SUBLANES), :]
        vs = swp_ref[pl.ds(r0, SUBLANES), :]
        t = lr * c + lx * cs + v
        ts = lr * cs - lx * c + vs
        t_sh = pltpu.roll(t, batch, 0)
        ts_sh = pltpu.roll(ts, batch, 0)
        t2 = lr * t_sh + lx * ts_sh + v
        t2s = lr * ts_sh - lx * t_sh + vs
        acc_ref[pl.ds(r0, SUBLANES), :] = jnp.where(first, c, t_sh)
        return (jnp.where(first, pltpu.roll(t2, batch, 0), t2),
                jnp.where(first, pltpu.roll(t2s, batch, 0), t2s))

    assert sub == 2 and (batch * nchunk) % SUBLANES == 0
    zero = jnp.zeros((SUBLANES, width), F32)
    lax.fori_loop(0, batch * nchunk // SUBLANES, step, (zero, zero))
    o_ref[...] = acc_ref[...].astype(BF16)


def _s5_state(u_im, e_c, lam, batch):
    c, rows, _ = u_im.shape
    return pl.pallas_call(
        functools.partial(_s5state_kernel, batch=batch, nchunk=rows // batch),
        grid=(S5_TILES,),
        in_specs=[pl.BlockSpec((c, rows, S5_COL_TILE), lambda t: (0, 0, t)),
                  pl.BlockSpec((c, S5_COL_TILE, S5_GROUP_STATE), lambda t: (0, t, 0)),
                  pl.BlockSpec((1, 2, S5_STATE_TILE), lambda t: (t, 0, 0))],
        out_specs=pl.BlockSpec((rows, S5_STATE_TILE), lambda t: (0, t)),
        out_shape=jax.ShapeDtypeStruct((rows, S5_TILES * S5_STATE_TILE), BF16),
        scratch_shapes=[pltpu.VMEM((c * S5_COL_TILE, S5_STATE_TILE), BF16),
                        pltpu.VMEM((rows, S5_STATE_TILE), F32),
                        pltpu.VMEM((rows, S5_STATE_TILE), F32)],
        compiler_params=_params(("arbitrary",)),
        name="s5state",
    )(u_im, e_c, lam)


def _s5out_kernel(u_ref, x_ref, f_ref, k_ref, o_ref, w_ref, lhs_ref):
    c = u_ref.shape[0]
    ct, gs, hg, st = S5_COL_TILE, S5_GROUP_STATE, S5_GROUP, S5_STATE_TILE

    @pl.when(pl.program_id(0) == 0)
    def _():
        w_ref[...] = jnp.zeros_like(w_ref)

    lane_group = lax.broadcasted_iota(jnp.int32, (gs, ct), 1) // hg
    for i in range(c):
        f_i = f_ref[i]
        for gl in range(S5_GROUPS_PER_TILE):
            w_ref[gl * gs:(gl + 1) * gs, i * ct:(i + 1) * ct] = jnp.where(
                lane_group == gl, f_i, jnp.zeros_like(f_i))
    lane_group = lax.broadcasted_iota(jnp.int32, (hg, ct), 1) // hg
    for j in range(c):
        k_j = k_ref[j]
        lag_block = jnp.concatenate(
            [jnp.where(lane_group == gl, k_j, jnp.zeros_like(k_j)) for gl in range(S5_GROUPS_PER_TILE)],
            axis=0)
        for i0 in range(c - j):
            w_ref[st + i0 * ct:st + (i0 + 1) * ct, (i0 + j) * ct:(i0 + j + 1) * ct] = lag_block

    lhs_ref[:, 0:st] = x_ref[...]
    for i in range(c):
        lhs_ref[:, st + i * ct:st + (i + 1) * ct] = u_ref[i]

    for n in range(c // 2):
        k = st + 2 * ct * (n + 1)
        acc = jnp.dot(lhs_ref[:, 0:k], w_ref[0:k, 2 * n * ct:2 * (n + 1) * ct], preferred_element_type=F32)
        y = jax.nn.gelu(acc).astype(BF16)
        o_ref[2 * n] = y[:, 0:ct]
        o_ref[2 * n + 1] = y[:, ct:2 * ct]


def _s5_out(u_im, x_prev, f_c, k_c):
    c, rows, width = u_im.shape
    return pl.pallas_call(
        _s5out_kernel,
        grid=(S5_TILES,),
        in_specs=[pl.BlockSpec((c, rows, S5_COL_TILE), lambda t: (0, 0, t)),
                  pl.BlockSpec((rows, S5_STATE_TILE), lambda t: (0, t)),
                  pl.BlockSpec((c, S5_GROUP_STATE, S5_COL_TILE), lambda t: (0, 0, t)),
                  pl.BlockSpec((c, S5_GROUP, S5_COL_TILE), lambda t: (0, 0, t))],
        out_specs=pl.BlockSpec((c, rows, S5_COL_TILE), lambda t: (0, 0, t)),
        out_shape=jax.ShapeDtypeStruct((c, rows, width), BF16),
        scratch_shapes=[pltpu.VMEM((S5_STATE_TILE + c * S5_COL_TILE, c * S5_COL_TILE), BF16),
                        pltpu.VMEM((rows, S5_STATE_TILE + c * S5_COL_TILE), BF16)],
        compiler_params=_params(("arbitrary",)),
        name="s5out",
    )(u_im, x_prev, f_c, k_c)


def _merge_kernel(ret_ref, ssm_ref, wro_ref, wga_ref, wgb_ref, gr_ref, gs_ref, o_ref):
    for k in range(o_ref.shape[1] // MXU_COLS):
        sl = slice(k * MXU_COLS, (k + 1) * MXU_COLS)
        y_ret = jnp.dot(ret_ref[...], wro_ref[:, sl], preferred_element_type=F32)
        glu_a = jnp.dot(ssm_ref[...], wga_ref[:, sl], preferred_element_type=F32)
        glu_b = jnp.dot(ssm_ref[...], wgb_ref[:, sl], preferred_element_type=F32)
        y_s5 = glu_a * _sigmoid(glu_b)
        merged = gr_ref[:, sl].astype(F32) * y_ret + gs_ref[:, sl].astype(F32) * y_s5
        o_ref[:, sl] = merged.astype(BF16)


def _merge(retn, yssm, proj, w_ret_out, w_s5_glu):
    t = retn.shape[0]
    tm, tn = 1024, 512
    nb = D_MODEL // tn
    return pl.pallas_call(
        _merge_kernel,
        grid=(t // tm, nb),
        in_specs=[pl.BlockSpec((tm, RET_V_WIDTH), lambda m, j: (m, 0)),
                  pl.BlockSpec((tm, S5_WIDTH), lambda m, j: (m, 0)),
                  pl.BlockSpec((RET_V_WIDTH, tn), lambda m, j: (0, j)),
                  pl.BlockSpec((S5_WIDTH, tn), lambda m, j: (0, j)),
                  pl.BlockSpec((S5_WIDTH, tn), lambda m, j: (0, nb + j)),
                  pl.BlockSpec((tm, tn), lambda m, j: (m, PROJ_GATE_R // tn + j)),
                  pl.BlockSpec((tm, tn), lambda m, j: (m, PROJ_GATE_S // tn + j))],
        out_specs=pl.BlockSpec((tm, tn), lambda m, j: (m, j)),
        out_shape=jax.ShapeDtypeStruct((t, D_MODEL), BF16),
        compiler_params=_params(("parallel", "arbitrary")),
        name="merge",
    )(retn, yssm, w_ret_out, w_s5_glu, w_s5_glu, proj, proj)


def _oproj_kernel(a_ref, w_ref, x_ref, mod_ref, g_ref, o_ref):
    y = jnp.dot(a_ref[...], w_ref[...], preferred_element_type=F32)
    ms = jnp.mean(y * y, axis=-1, keepdims=True)
    yn = y * lax.rsqrt(ms + NORM_EPS) * g_ref[...]
    o_ref[...] = x_ref[...] + mod_ref[0, 2:3, :] * yn


def _oproj(merged, w_out, x2, mod, gain, seq):
    t = merged.shape[0]
    tm = 512
    tiles_per_seq = seq // tm
    return pl.pallas_call(
        _oproj_kernel,
        grid=(t // tm,),
        in_specs=[pl.BlockSpec((tm, D_MODEL), lambda m: (m, 0)),
                  pl.BlockSpec((D_MODEL, D_MODEL), lambda m: (0, 0)),
                  pl.BlockSpec((tm, D_MODEL), lambda m: (m, 0)),
                  pl.BlockSpec((1, 6, D_MODEL), lambda m: (m // tiles_per_seq, 0, 0)),
                  pl.BlockSpec((1, D_MODEL), lambda m: (0, 0))],
        out_specs=pl.BlockSpec((tm, D_MODEL), lambda m: (m, 0)),
        out_shape=jax.ShapeDtypeStruct((t, D_MODEL), F32),
        compiler_params=_params(("parallel",)),
        name="oproj",
    )(merged, w_out, x2, mod, gain)


def _ffn_kernel(x_ref, mod_ref, g2_ref, g3_ref, wa_ref, wb_ref, wo_ref, o_ref, h_ref, act_ref, ab_ref,
                *, nf, no):
    j = pl.program_id(1)
    tf = wa_ref.shape[1]
    tno = wo_ref.shape[1]
    col_tiles = [slice(k * MXU_COLS, (k + 1) * MXU_COLS) for k in range(tf // MXU_COLS)]

    def activate(tile, sl):
        act_ref[tile, :, sl] = (_silu(ab_ref[0, :, sl]) * ab_ref[1, :, sl]).astype(BF16)

    def up_project(sl):
        ab_ref[0, :, sl] = jnp.dot(h_ref[...], wa_ref[:, sl], preferred_element_type=F32)
        ab_ref[1, :, sl] = jnp.dot(h_ref[...], wb_ref[:, sl], preferred_element_type=F32)

    def down_project(n):
        y = jnp.dot(act_ref[0], wo_ref[0:tf, :], preferred_element_type=F32)
        for jj in range(1, nf):
            y += jnp.dot(act_ref[jj], wo_ref[jj * tf:(jj + 1) * tf, :], preferred_element_type=F32)
        o_ref[:, n * tno:(n + 1) * tno] = y

    @pl.when(j == 0)
    def _():
        x = x_ref[...]
        ms = jnp.mean(x * x, axis=-1, keepdims=True)
        y = x * lax.rsqrt(ms + NORM_EPS) * g2_ref[...]
        h_ref[...] = (y * (1.0 + mod_ref[0, 4:5, :]) + mod_ref[0, 3:4, :]).astype(BF16)
        for sl in col_tiles:
            up_project(sl)

    @pl.when((j > 0) & (j < nf))
    def _():
        for sl in col_tiles:
            activate(j - 1, sl)
            up_project(sl)

    @pl.when(j == nf)
    def _():
        for sl in col_tiles:
            activate(nf - 1, sl)
        down_project(0)

    for n in range(1, no):
        @pl.when(j == nf + n)
        def _(n=n):
            down_project(n)

    @pl.when(j == nf + no - 1)
    def _():
        y = o_ref[...]
        ms = jnp.mean(y * y, axis=-1, keepdims=True)
        yn = y * lax.rsqrt(ms + NORM_EPS) * g3_ref[...]
        o_ref[...] = x_ref[...] + mod_ref[0, 5:6, :] * yn


def _ffn(x1, mod, gain2, gain3, w_ffn_in, w_ffn_out, seq):
    t = x1.shape[0]
    tm, tf, tno = 512, 512, 512
    nf = D_FF // tf
    no = D_MODEL // tno
    tiles_per_seq = seq // tm
    return pl.pallas_call(
        functools.partial(_ffn_kernel, nf=nf, no=no),
        grid=(t // tm, nf + no),
        in_specs=[pl.BlockSpec((tm, D_MODEL), lambda m, j: (m, 0)),
                  pl.BlockSpec((1, 6, D_MODEL), lambda m, j: (m // tiles_per_seq, 0, 0)),
                  pl.BlockSpec((1, D_MODEL), lambda m, j: (0, 0)),
                  pl.BlockSpec((1, D_MODEL), lambda m, j: (0, 0)),
                  pl.BlockSpec((D_MODEL, tf), lambda m, j: (0, jnp.minimum(j, nf - 1))),
                  pl.BlockSpec((D_MODEL, tf), lambda m, j: (0, nf + jnp.minimum(j, nf - 1))),
                  pl.BlockSpec((D_FF, tno), lambda m, j: (0, jnp.maximum(j - nf, 0)))],
        out_specs=pl.BlockSpec((tm, D_MODEL), lambda m, j: (m, 0)),
        out_shape=jax.ShapeDtypeStruct((t, D_MODEL), F32),
        scratch_shapes=[pltpu.VMEM((tm, D_MODEL), BF16), pltpu.VMEM((nf, tm, tf), BF16),
                        pltpu.VMEM((2, tm, tf), F32)],
        compiler_params=_params(("parallel", "arbitrary")),
        name="ffn",
    )(x1, mod, gain2, gain3, w_ffn_in, w_ffn_in, w_ffn_out)


def _rope_tables(seq):
    pos = jnp.arange(seq, dtype=F32)
    inv_freq = ROPE_BASE ** (-jnp.arange(RET_QK_DIM // 2, dtype=F32) * (2.0 / RET_QK_DIM))
    ang = pos[:, None] * inv_freq[None, :]
    cos, sin = jnp.cos(ang), jnp.sin(ang)
    return jnp.concatenate([cos, cos], axis=-1), jnp.concatenate([-sin, sin], axis=-1)


def _s5_branch(u, params, batch, seq):
    a_re, a_im, log_dt, b_re, b_im, c_re, c_im, d_skip = params
    t = u.shape[0]
    rows = t // S5_CHUNK
    e_c, f_c, k_c, lam = _s5_compact(*_s5_prep(a_re, a_im, log_dt, b_re, b_im, c_re, c_im, d_skip))
    nchunk = seq // S5_CHUNK
    u = u.reshape(batch, nchunk, S5_CHUNK, S5_WIDTH)
    u_im = u.transpose(2, 1, 0, 3).reshape(S5_CHUNK, rows, S5_WIDTH)
    x_prev = _s5_state(u_im, e_c, lam, batch)
    y_im = _s5_out(u_im, x_prev, f_c, k_c).reshape(S5_CHUNK, nchunk, batch, S5_WIDTH)
    return y_im.transpose(2, 1, 0, 3).reshape(t, S5_WIDTH)


def kernel(x, c, w_ada, b_ada, norm_gains, w_in, w_ret_out, ssm_a_re, ssm_a_im, ssm_log_dt,
           ssm_b_re, ssm_b_im, ssm_c_re, ssm_c_im, ssm_d, w_s5_glu, w_out, w_ffn_in, w_ffn_out):
    batch, seq, _ = x.shape
    depth = w_in.shape[0]
    cosf, sinf = _rope_tables(seq)
    xt = x.reshape(batch * seq, D_MODEL)
    for l in range(depth):
        gains = norm_gains[l][:, None, :]
        mod = _adaln_mod(c, w_ada[l], b_ada[l])
        proj, u = _inproj(xt, mod, gains[0], w_in[l].astype(BF16), cosf, sinf, seq)
        retn = _retention(proj, batch, seq)
        yssm = _s5_branch(u, (ssm_a_re[l], ssm_a_im[l], ssm_log_dt[l], ssm_b_re[l], ssm_b_im[l],
                              ssm_c_re[l], ssm_c_im[l], ssm_d[l]), batch, seq)
        merged = _merge(retn, yssm, proj, w_ret_out[l].astype(BF16), w_s5_glu[l].astype(BF16))
        x1 = _oproj(merged, w_out[l].astype(BF16), xt, mod, gains[1], seq)
        xt = _ffn(x1, mod, gains[2], gains[3], w_ffn_in[l].astype(BF16), w_ffn_out[l].astype(BF16), seq)
    return xt.reshape(batch, seq, D_MODEL)
```

```python
import functools
import math

import jax
import jax.numpy as jnp
from jax import lax
from jax.experimental import pallas as pl
from jax.experimental.pallas import tpu as pltpu

F32 = jnp.float32
BF16 = jnp.bfloat16

D_MODEL = 2048
RET_HEADS = 8
RET_QK_DIM = 128
RET_V_DIM = 256
RET_QK_WIDTH = RET_HEADS * RET_QK_DIM
RET_V_WIDTH = RET_HEADS * RET_V_DIM
ROPE_BASE = 10000.0
S5_GROUP = 16
S5_WIDTH = D_MODEL // 2
S5_GROUPS = S5_WIDTH // S5_GROUP
S5_STATE = 64
D_FF = 5632
NORM_EPS = 1e-6
GN_EPS = 1e-5

IN_SIZES = (RET_QK_WIDTH, RET_QK_WIDTH, RET_V_WIDTH, RET_V_WIDTH, S5_WIDTH, D_MODEL, D_MODEL)
IN_WIDTH = sum(IN_SIZES)
COL_Q, COL_K, COL_V, COL_GRET, COL_U, COL_GATE_R, COL_GATE_S = (
    sum(IN_SIZES[:i]) for i in range(len(IN_SIZES)))
RETIN_WIDTH = COL_U
GATES_WIDTH = IN_WIDTH - COL_GATE_R

V7X_VMEM_BYTES = 64 * 1024 * 1024
VMEM_LIMIT = V7X_VMEM_BYTES - 8 * 1024 * 1024
MXU_COLS = 256
SUBLANES = 8
LANES = 128

S5_CHUNK = 16
S5_COL_TILE = 128
S5_GROUPS_PER_TILE = S5_COL_TILE // S5_GROUP
S5_GROUP_STATE = 2 * S5_STATE
S5_STATE_TILE = S5_GROUPS_PER_TILE * S5_GROUP_STATE
S5_TILES = S5_WIDTH // S5_COL_TILE
RET_BLOCK = 256


def _params(semantics):
    return pltpu.CompilerParams(dimension_semantics=semantics, vmem_limit_bytes=VMEM_LIMIT)


def _sigmoid(x):
    return 0.5 * jnp.tanh(0.5 * x) + 0.5


def _silu(x):
    half = 0.5 * x
    return half + half * jnp.tanh(half)


def _mod_kernel(c_ref, w_ref, b_ref, o_ref):
    @pl.when(pl.program_id(0) == 0)
    def _():
        o_ref[...] = jnp.broadcast_to(b_ref[...], o_ref.shape)

    c = c_ref[...]
    ca = (c * _sigmoid(c)).astype(BF16)
    o_ref[...] += jnp.dot(ca, w_ref[...].astype(BF16), preferred_element_type=F32)


def _adaln_mod(c, w_ada, b_ada):
    bn = c.shape[0]
    rows = 8
    c_pad = jnp.zeros((rows, D_MODEL), F32).at[:bn].set(c)
    tk = 256
    out = pl.pallas_call(
        _mod_kernel,
        grid=(D_MODEL // tk,),
        in_specs=[pl.BlockSpec((rows, tk), lambda k: (0, k)),
                  pl.BlockSpec((tk, 6 * D_MODEL), lambda k: (k, 0)),
                  pl.BlockSpec((1, 6 * D_MODEL), lambda k: (0, 0))],
        out_specs=pl.BlockSpec((rows, 6 * D_MODEL), lambda k: (0, 0)),
        out_shape=jax.ShapeDtypeStruct((rows, 6 * D_MODEL), F32),
        compiler_params=_params(("arbitrary",)),
        name="mod",
    )(c_pad, w_ada, b_ada.reshape(1, -1))
    return out[:bn].reshape(bn, 6, D_MODEL)


def _inproj_kernel(x_ref, mod_ref, g_ref, w_ref, cos_ref, sin_ref, o_ref, u_ref, gate_ref, h_ref, stage_ref):
    j = pl.program_id(1)

    @pl.when(j == 0)
    def _():
        x = x_ref[...]
        ms = jnp.mean(x * x, axis=-1, keepdims=True)
        y = x * lax.rsqrt(ms + NORM_EPS) * g_ref[...]
        h = y * (1.0 + mod_ref[0, 1:2, :]) + mod_ref[0, 0:1, :]
        h_ref[...] = h.astype(BF16)

    tn = w_ref.shape[1]
    q_tiles = COL_K // tn

    def project(epilogue, out_ref):
        for k in range(tn // MXU_COLS):
            sl = slice(k * MXU_COLS, (k + 1) * MXU_COLS)
            acc = jnp.dot(h_ref[...], w_ref[:, sl], preferred_element_type=F32)
            out_ref[:, sl] = epilogue(acc).astype(BF16)

    @pl.when(j < COL_V // tn)
    def _():
        scale = jnp.where(j >= q_tiles, RET_QK_DIM ** -0.5, 1.0).astype(F32)
        cosf = cos_ref[...] * scale
        sinf = sin_ref[...] * scale

        def rope(acc):
            heads = []
            for hh in range(MXU_COLS // RET_QK_DIM):
                t = acc[:, hh * RET_QK_DIM:(hh + 1) * RET_QK_DIM]
                heads.append(t * cosf + pltpu.roll(t, RET_QK_DIM // 2, 1) * sinf)
            return jnp.concatenate(heads, axis=1)

        project(rope, o_ref)

    @pl.when((j >= COL_V // tn) & (j < COL_GRET // tn))
    def _():
        project(lambda acc: acc, o_ref)

    @pl.when((j >= COL_GRET // tn) & (j < COL_U // tn))
    def _():
        project(_silu, o_ref)

    @pl.when((j >= COL_U // tn) & (j < COL_GATE_R // tn))
    def _():
        chunks = x_ref.shape[0] // S5_CHUNK
        for k in range(tn // MXU_COLS):
            acc = jnp.dot(h_ref[...], w_ref[:, k * MXU_COLS:(k + 1) * MXU_COLS], preferred_element_type=F32)
            for part in range(MXU_COLS // LANES):
                stage_ref[part] = acc[:, part * LANES:(part + 1) * LANES]
            for i in range(S5_CHUNK):
                for part in range(MXU_COLS // LANES):
                    lo = k * MXU_COLS + part * LANES
                    u_ref[i, :, lo:lo + LANES] = (
                        stage_ref[part, pl.ds(i, chunks, stride=S5_CHUNK), :].astype(BF16))

    @pl.when(j >= COL_GATE_R // tn)
    def _():
        project(_sigmoid, gate_ref)


def _inproj(x2, mod, gain, w_in, cosf, sinf, seq):
    t = x2.shape[0]
    tm, tn = 1024, 1024
    assert S5_WIDTH == tn
    ret_tiles = RETIN_WIDTH // tn
    gate_tile0 = COL_GATE_R // tn
    gate_tiles = GATES_WIDTH // tn
    tiles_per_seq = seq // tm
    batch = t // seq
    return pl.pallas_call(
        _inproj_kernel,
        grid=(t // tm, IN_WIDTH // tn),
        in_specs=[pl.BlockSpec((tm, D_MODEL), lambda m, j: (m, 0)),
                  pl.BlockSpec((1, 6, D_MODEL), lambda m, j: (m // tiles_per_seq, 0, 0)),
                  pl.BlockSpec((1, D_MODEL), lambda m, j: (0, 0)),
                  pl.BlockSpec((D_MODEL, tn), lambda m, j: (0, j)),
                  pl.BlockSpec((tm, RET_QK_DIM), lambda m, j: (m % tiles_per_seq, 0)),
                  pl.BlockSpec((tm, RET_QK_DIM), lambda m, j: (m % tiles_per_seq, 0))],
        out_specs=[pl.BlockSpec((tm, tn), lambda m, j: (m, jnp.minimum(j, ret_tiles - 1))),
                   pl.BlockSpec((S5_CHUNK, tm // S5_CHUNK, tn),
                                lambda m, j: (0, m % tiles_per_seq, m // tiles_per_seq)),
                   pl.BlockSpec((tm, tn), lambda m, j: (m, jnp.clip(j - gate_tile0, 0, gate_tiles - 1)))],
        out_shape=[jax.ShapeDtypeStruct((t, RETIN_WIDTH), BF16),
                   jax.ShapeDtypeStruct((S5_CHUNK, seq // S5_CHUNK, batch * S5_WIDTH), BF16),
                   jax.ShapeDtypeStruct((t, GATES_WIDTH), BF16)],
        scratch_shapes=[pltpu.VMEM((tm, D_MODEL), BF16), pltpu.VMEM((MXU_COLS // LANES, tm, LANES), F32)],
        compiler_params=_params(("parallel", "arbitrary")),
        name="inproj",
    )(x2, mod, gain, w_in, cosf, sinf)


def _ret_kernel(x_ref, o_ref, state_ref, decay_ref, wcross_ref, wstate_ref):
    n = pl.program_id(1)
    cr = x_ref.shape[0]

    @pl.when(n == 0)
    def _():
        state_ref[...] = jnp.zeros_like(state_ref)

    @pl.when((pl.program_id(0) == 0) & (n == 0))
    def _():
        row = lax.broadcasted_iota(jnp.int32, (cr, cr), 0)
        col = lax.broadcasted_iota(jnp.int32, (cr, cr), 1)
        rel = (row - col).astype(F32)
        idx_v = lax.broadcasted_iota(jnp.int32, (cr, RET_V_DIM), 0).astype(F32)
        idx_k = lax.broadcasted_iota(jnp.int32, (cr, RET_QK_DIM), 0).astype(F32)
        for h in range(RET_HEADS):
            log_g = math.log1p(-(2.0 ** (-5.0 - h)))
            decay_ref[h] = jnp.where(rel >= 0.0, jnp.exp(jnp.maximum(rel, 0.0) * log_g), 0.0)
            wcross_ref[h] = jnp.exp((idx_v + 1.0) * log_g)
            wstate_ref[h] = jnp.exp((cr - 1.0 - idx_k) * log_g)

    for h in range(RET_HEADS):
        chunk_decay = math.exp(cr * math.log1p(-(2.0 ** (-5.0 - h))))
        vs = slice(h * RET_V_DIM, (h + 1) * RET_V_DIM)
        q = x_ref[:, COL_Q + h * RET_QK_DIM:COL_Q + (h + 1) * RET_QK_DIM]
        k = x_ref[:, COL_K + h * RET_QK_DIM:COL_K + (h + 1) * RET_QK_DIM]
        v = x_ref[:, COL_V + h * RET_V_DIM:COL_V + (h + 1) * RET_V_DIM]
        gate = x_ref[:, COL_GRET + h * RET_V_DIM:COL_GRET + (h + 1) * RET_V_DIM]
        scores = lax.dot_general(q, k, (((1,), (1,)), ((), ())), preferred_element_type=F32)
        inner = jnp.dot((scores * decay_ref[h]).astype(BF16), v, preferred_element_type=F32)
        state = state_ref[h]
        cross = jnp.dot(q, state.astype(BF16), preferred_element_type=F32) * wcross_ref[h]
        ret = inner + cross
        mu = jnp.mean(ret, axis=-1, keepdims=True)
        dev = ret - mu
        var = jnp.mean(dev * dev, axis=-1, keepdims=True)
        normed = dev * lax.rsqrt(var + GN_EPS)
        o_ref[:, vs] = (normed * gate.astype(F32)).astype(BF16)

        kw = (k.astype(F32) * wstate_ref[h]).astype(BF16)
        kv = lax.dot_general(kw, v, (((0,), (0,)), ((), ())), preferred_element_type=F32)
        state_ref[h] = chunk_decay * state + kv


def _retention(proj, batch, seq):
    t = proj.shape[0]
    cr = RET_BLOCK
    nblk = seq // cr
    return pl.pallas_call(
        _ret_kernel,
        grid=(batch, nblk),
        in_specs=[pl.BlockSpec((cr, RETIN_WIDTH), lambda b, n: (b * nblk + n, 0))],
        out_specs=pl.BlockSpec((cr, RET_V_WIDTH), lambda b, n: (b * nblk + n, 0)),
        out_shape=jax.ShapeDtypeStruct((t, RET_V_WIDTH), BF16),
        scratch_shapes=[pltpu.VMEM((RET_HEADS, RET_QK_DIM, RET_V_DIM), F32),
                        pltpu.VMEM((RET_HEADS, cr, cr), F32),
                        pltpu.VMEM((RET_HEADS, cr, RET_V_DIM), F32),
                        pltpu.VMEM((RET_HEADS, cr, RET_QK_DIM), F32)],
        compiler_params=_params(("arbitrary", "arbitrary")),
        name="ret",
    )(proj)


def _cmul(ar, ai, br, bi):
    return ar * br - ai * bi, ar * bi + ai * br


def _s5prep_kernel(are_ref, aim_ref, ldt_ref, bre_ref, bim_ref, cre_ref, cim_ref, d_ref,
                   er_ref, ei_ref, fr_ref, fi_ref, k_ref, lr_ref, li_ref):
    ar = are_ref[...]
    ai = aim_ref[...]
    dt = jnp.exp(ldt_ref[...])
    mag = jnp.exp(ar * dt)
    lbr = mag * jnp.cos(ai * dt)
    lbi = mag * jnp.sin(ai * dt)
    nr = lbr - 1.0
    den = ar * ar + ai * ai
    f_re = (nr * ar + lbi * ai) / den
    f_im = (lbi * ar - nr * ai) / den
    bbr, bbi = _cmul(f_re, f_im, bre_ref[...], bim_ref[...])

    rows = S5_CHUNK * S5_GROUP
    gb, _, p = ar.shape
    lag = lax.broadcasted_iota(jnp.int32, (gb, S5_CHUNK, p), 1).astype(F32)

    def lam_pow(k):
        m = jnp.exp(ar * dt * k)
        return m * jnp.cos(ai * dt * k), m * jnp.sin(ai * dt * k)

    def tile_rows(a):
        return jnp.concatenate([a] * S5_CHUNK, axis=1)

    def repeat_rows(a):
        return jnp.concatenate(
            [jnp.broadcast_to(a[:, k:k + 1, :], (gb, S5_GROUP, p)) for k in range(S5_CHUNK)], axis=1)

    p0r, p0i = lam_pow(lag)
    p1r, p1i = _cmul(p0r, p0i, lbr, lbi)
    er, ei = _cmul(repeat_rows(p0r), repeat_rows(p0i), tile_rows(bbr), tile_rows(bbi))
    er_ref[...] = er
    ei_ref[...] = ei
    fr, fi = _cmul(repeat_rows(p1r), repeat_rows(p1i),
                   tile_rows(cre_ref[...]), tile_rows(cim_ref[...]))
    fr_ref[...] = fr
    fi_ref[...] = -fi

    dims = (((2,), (2,)), ((0,), (0,)))
    kern = (lax.dot_general(cre_ref[...], er, dims, preferred_element_type=F32,
                            precision=lax.Precision.HIGHEST)
            - lax.dot_general(cim_ref[...], ei, dims, preferred_element_type=F32,
                              precision=lax.Precision.HIGHEST))
    hrow = lax.broadcasted_iota(jnp.int32, (1, S5_GROUP, rows), 1)
    ncol = lax.broadcasted_iota(jnp.int32, (1, S5_GROUP, rows), 2)
    k_ref[...] = kern + jnp.where(hrow == ncol, d_ref[...], 0.0)

    lcr, lci = lam_pow(float(S5_CHUNK))
    lr_ref[...] = lcr
    li_ref[...] = lci


def _s5_prep(a_re, a_im, log_dt, b_re, b_im, c_re, c_im, d_skip):
    g, p, hg = S5_GROUPS, S5_STATE, S5_GROUP
    gb = 8
    rows = S5_CHUNK * hg
    gp = lambda: pl.BlockSpec((gb, 1, p), lambda i: (i, 0, 0))
    ghp = lambda: pl.BlockSpec((gb, hg, p), lambda i: (i, 0, 0))
    big = lambda: pl.BlockSpec((gb, rows, p), lambda i: (i, 0, 0))
    outs = pl.pallas_call(
        _s5prep_kernel,
        grid=(g // gb,),
        in_specs=[gp(), gp(), gp(), ghp(), ghp(), ghp(), ghp(),
                  pl.BlockSpec((gb, hg, 1), lambda i: (i, 0, 0))],
        out_specs=[big(), big(), big(), big(),
                   pl.BlockSpec((gb, hg, rows), lambda i: (i, 0, 0)), gp(), gp()],
        out_shape=[jax.ShapeDtypeStruct((g, rows, p), F32)] * 4
        + [jax.ShapeDtypeStruct((g, hg, rows), F32)]
        + [jax.ShapeDtypeStruct((g, 1, p), F32)] * 2,
        compiler_params=_params(("arbitrary",)),
        name="s5prep",
    )(a_re[:, None, :], a_im[:, None, :],
      jnp.broadcast_to(log_dt[:, None, None], (g, 1, p)),
      jnp.swapaxes(b_re, 1, 2), jnp.swapaxes(b_im, 1, 2), c_re, c_im, d_skip[:, :, None])
    return outs


def _s5_compact(er, ei, fr, fin, kern, lcr, lci):
    g, hg, p, c = S5_GROUPS, S5_GROUP, S5_STATE, S5_CHUNK
    e = jnp.concatenate([er, ei], axis=-1).reshape(g, c, hg, 2 * p)
    e_c = jnp.flip(e.transpose(1, 0, 2, 3), axis=0).reshape(c, g * hg, 2 * p).astype(BF16)
    f = jnp.concatenate([fr, fin], axis=-1).reshape(g, c, hg, 2 * p)
    f_c = f.transpose(1, 3, 0, 2).reshape(c, 2 * p, g * hg).astype(BF16)
    k_c = kern.reshape(g, hg, c, hg).transpose(2, 3, 0, 1).reshape(c, hg, g * hg).astype(BF16)
    lam = jnp.stack([jnp.concatenate([lcr, lcr], axis=-1).reshape(S5_TILES, S5_STATE_TILE),
                     jnp.concatenate([lci, lci], axis=-1).reshape(S5_TILES, S5_STATE_TILE)], axis=1)
    return e_c, f_c, k_c, lam


def _s5state_kernel(u_ref, e_ref, lam_ref, o_ref, w_ref, acc_ref, swp_ref, *, batch, nchunk):
    c = u_ref.shape[0]
    ct, gs, hg = S5_COL_TILE, S5_GROUP_STATE, S5_GROUP

    @pl.when(pl.program_id(0) == 0)
    def _():
        w_ref[...] = jnp.zeros_like(w_ref)

    for i in range(c):
        for gl in range(S5_GROUPS_PER_TILE):
            w_ref[i * ct + gl * hg:i * ct + (gl + 1) * hg, gl * gs:(gl + 1) * gs] = (
                e_ref[i, gl * hg:(gl + 1) * hg, :])

    lhs = jnp.concatenate([u_ref[i] for i in range(c)], axis=1)
    s = jnp.dot(lhs, w_ref[...], preferred_element_type=F32)
    width = s.shape[1]
    is_re = (lax.broadcasted_iota(jnp.int32, (1, width), 1) % gs) < S5_STATE
    acc_ref[...] = s
    swp_ref[...] = jnp.where(is_re, pltpu.roll(s, width - S5_STATE, 1), pltpu.roll(s, S5_STATE, 1))

    lr = lam_ref[0, 0:1, :]
    li = lam_ref[0, 1:2, :]
    lx = jnp.where(is_re, -li, li)

    sub = SUBLANES // batch
    first = lax.broadcasted_iota(jnp.int32, (SUBLANES, width), 0) < batch

    def step(m, carry):
        c, cs = carry
        r0 = pl.multiple_of(m * SUBLANES, SUBLANES)
        v = acc_ref[pl.ds(r0, SUBLANES), :]
        vs = swp_ref[pl.ds(r0, SUBLANES), :]
        t = lr * c + lx * cs + v
        ts = lr * cs - lx * c + vs
        t_sh = pltpu.roll(t, batch, 0)
        ts_sh = pltpu.roll(ts, batch, 0)
        t2 = lr * t_sh + lx * ts_sh + v
        t2s = lr * ts_sh - lx * t_sh + vs
        acc_ref[pl.ds(r0, SUBLANES), :] = jnp.where(first, c, t_sh)
        return (jnp.where(first, pltpu.roll(t2, batch, 0), t2),
                jnp.where(first, pltpu.roll(t2s, batch, 0), t2s))

    assert sub == 2 and (batch * nchunk) % SUBLANES == 0
    zero = jnp.zeros((SUBLANES, width), F32)
    lax.fori_loop(0, batch * nchunk // SUBLANES, step, (zero, zero))
    o_ref[...] = acc_ref[...].astype(BF16)


def _s5_state(u_im, e_c, lam, batch):
    c, rows, _ = u_im.shape
    return pl.pallas_call(
        functools.partial(_s5state_kernel, batch=batch, nchunk=rows // batch),
        grid=(S5_TILES,),
        in_specs=[pl.BlockSpec((c, rows, S5_COL_TILE), lambda t: (0, 0, t)),
                  pl.BlockSpec((c, S5_COL_TILE, S5_GROUP_STATE), lambda t: (0, t, 0)),
                  pl.BlockSpec((1, 2, S5_STATE_TILE), lambda t: (t, 0, 0))],
        out_specs=pl.BlockSpec((rows, S5_STATE_TILE), lambda t: (0, t)),
        out_shape=jax.ShapeDtypeStruct((rows, S5_TILES * S5_STATE_TILE), BF16),
        scratch_shapes=[pltpu.VMEM((c * S5_COL_TILE, S5_STATE_TILE), BF16),
                        pltpu.VMEM((rows, S5_STATE_TILE), F32),
                        pltpu.VMEM((rows, S5_STATE_TILE), F32)],
        compiler_params=_params(("arbitrary",)),
        name="s5state",
    )(u_im, e_c, lam)


def _s5out_kernel(u_ref, x_ref, f_ref, k_ref, o_ref, w_ref, lhs_ref):
    c = u_ref.shape[0]
    ct, gs, hg, st = S5_COL_TILE, S5_GROUP_STATE, S5_GROUP, S5_STATE_TILE

    @pl.when(pl.program_id(0) == 0)
    def _():
        w_ref[...] = jnp.zeros_like(w_ref)

    lane_group = lax.broadcasted_iota(jnp.int32, (gs, ct), 1) // hg
    for i in range(c):
        f_i = f_ref[i]
        for gl in range(S5_GROUPS_PER_TILE):
            w_ref[gl * gs:(gl + 1) * gs, i * ct:(i + 1) * ct] = jnp.where(
                lane_group == gl, f_i, jnp.zeros_like(f_i))
    lane_group = lax.broadcasted_iota(jnp.int32, (hg, ct), 1) // hg
    for j in range(c):
        k_j = k_ref[j]
        lag_block = jnp.concatenate(
            [jnp.where(lane_group == gl, k_j, jnp.zeros_like(k_j)) for gl in range(S5_GROUPS_PER_TILE)],
            axis=0)
        for i0 in range(c - j):
            w_ref[st + i0 * ct:st + (i0 + 1) * ct, (i0 + j) * ct:(i0 + j + 1) * ct] = lag_block

    lhs_ref[:, 0:st] = x_ref[...]
    for i in range(c):
        lhs_ref[:, st + i * ct:st + (i + 1) * ct] = u_ref[i]

    for n in range(c // 2):
        k = st + 2 * ct * (n + 1)
        acc = jnp.dot(lhs_ref[:, 0:k], w_ref[0:k, 2 * n * ct:2 * (n + 1) * ct], preferred_element_type=F32)
        y = jax.nn.gelu(acc).astype(BF16)
        o_ref[2 * n] = y[:, 0:ct]
        o_ref[2 * n + 1] = y[:, ct:2 * ct]


def _s5_out(u_im, x_prev, f_c, k_c):
    c, rows, width = u_im.shape
    return pl.pallas_call(
        _s5out_kernel,
        grid=(S5_TILES,),
        in_specs=[pl.BlockSpec((c, rows, S5_COL_TILE), lambda t: (0, 0, t)),
                  pl.BlockSpec((rows, S5_STATE_TILE), lambda t: (0, t)),
                  pl.BlockSpec((c, S5_GROUP_STATE, S5_COL_TILE), lambda t: (0, 0, t)),
                  pl.BlockSpec((c, S5_GROUP, S5_COL_TILE), lambda t: (0, 0, t))],
        out_specs=pl.BlockSpec((c, rows, S5_COL_TILE), lambda t: (0, 0, t)),
        out_shape=jax.ShapeDtypeStruct((c, rows, width), BF16),
        scratch_shapes=[pltpu.VMEM((S5_STATE_TILE + c * S5_COL_TILE, c * S5_COL_TILE), BF16),
                        pltpu.VMEM((rows, S5_STATE_TILE + c * S5_COL_TILE), BF16)],
        compiler_params=_params(("arbitrary",)),
        name="s5out",
    )(u_im, x_prev, f_c, k_c)


def _merge_kernel(ret_ref, ssm_ref, wro_ref, wga_ref, wgb_ref, gr_ref, gs_ref, o_ref, stage_ref):
    c, chunks, width = ssm_ref.shape
    ssm = ssm_ref[...].reshape(c * chunks, width)
    for k in range(o_ref.shape[1] // MXU_COLS):
        sl = slice(k * MXU_COLS, (k + 1) * MXU_COLS)
        y_ret = jnp.dot(ret_ref[...], wro_ref[:, sl], preferred_element_type=F32)
        glu_a = jnp.dot(ssm, wga_ref[:, sl], preferred_element_type=F32)
        glu_b = jnp.dot(ssm, wgb_ref[:, sl], preferred_element_type=F32)
        y_s5 = glu_a * _sigmoid(glu_b)
        for part in range(MXU_COLS // LANES):
            for i in range(c):
                stage_ref[part, pl.ds(i, chunks, stride=c), :] = (
                    y_s5[i * chunks:(i + 1) * chunks, part * LANES:(part + 1) * LANES])
        y_s5 = jnp.concatenate([stage_ref[part] for part in range(MXU_COLS // LANES)], axis=1)
        merged = gr_ref[:, sl].astype(F32) * y_ret + gs_ref[:, sl].astype(F32) * y_s5
        o_ref[:, sl] = merged.astype(BF16)


def _merge(retn, y_im, gates, w_ret_out, w_s5_glu, seq):
    t = retn.shape[0]
    tm, tn = 1024, 512
    nb = D_MODEL // tn
    tiles_per_seq = seq // tm
    return pl.pallas_call(
        _merge_kernel,
        grid=(t // tm, nb),
        in_specs=[pl.BlockSpec((tm, RET_V_WIDTH), lambda m, j: (m, 0)),
                  pl.BlockSpec((S5_CHUNK, tm // S5_CHUNK, S5_WIDTH),
                               lambda m, j: (0, m % tiles_per_seq, m // tiles_per_seq)),
                  pl.BlockSpec((RET_V_WIDTH, tn), lambda m, j: (0, j)),
                  pl.BlockSpec((S5_WIDTH, tn), lambda m, j: (0, j)),
                  pl.BlockSpec((S5_WIDTH, tn), lambda m, j: (0, nb + j)),
                  pl.BlockSpec((tm, tn), lambda m, j: (m, j)),
                  pl.BlockSpec((tm, tn), lambda m, j: (m, nb + j))],
        out_specs=pl.BlockSpec((tm, tn), lambda m, j: (m, j)),
        out_shape=jax.ShapeDtypeStruct((t, D_MODEL), BF16),
        scratch_shapes=[pltpu.VMEM((MXU_COLS // LANES, tm, LANES), F32)],
        compiler_params=_params(("parallel", "arbitrary")),
        name="merge",
    )(retn, y_im, w_ret_out, w_s5_glu, w_s5_glu, gates, gates)


def _oproj_kernel(a_ref, w_ref, x_ref, mod_ref, g_ref, o_ref):
    y = jnp.dot(a_ref[...], w_ref[...], preferred_element_type=F32)
    ms = jnp.mean(y * y, axis=-1, keepdims=True)
    yn = y * lax.rsqrt(ms + NORM_EPS) * g_ref[...]
    o_ref[...] = x_ref[...] + mod_ref[0, 2:3, :] * yn


def _oproj(merged, w_out, x2, mod, gain, seq):
    t = merged.shape[0]
    tm = 512
    tiles_per_seq = seq // tm
    return pl.pallas_call(
        _oproj_kernel,
        grid=(t // tm,),
        in_specs=[pl.BlockSpec((tm, D_MODEL), lambda m: (m, 0)),
                  pl.BlockSpec((D_MODEL, D_MODEL), lambda m: (0, 0)),
                  pl.BlockSpec((tm, D_MODEL), lambda m: (m, 0)),
                  pl.BlockSpec((1, 6, D_MODEL), lambda m: (m // tiles_per_seq, 0, 0)),
                  pl.BlockSpec((1, D_MODEL), lambda m: (0, 0))],
        out_specs=pl.BlockSpec((tm, D_MODEL), lambda m: (m, 0)),
        out_shape=jax.ShapeDtypeStruct((t, D_MODEL), F32),
        compiler_params=_params(("parallel",)),
        name="oproj",
    )(merged, w_out, x2, mod, gain)


def _ffn_kernel(x_ref, mod_ref, g2_ref, g3_ref, wa_ref, wb_ref, wo_ref, o_ref, h_ref, act_ref, ab_ref,
                *, nf, no):
    j = pl.program_id(1)
    tf = wa_ref.shape[1]
    tno = wo_ref.shape[1]
    col_tiles = [slice(k * MXU_COLS, (k + 1) * MXU_COLS) for k in range(tf // MXU_COLS)]

    def activate(tile, sl):
        act_ref[tile, :, sl] = (_silu(ab_ref[0, :, sl]) * ab_ref[1, :, sl]).astype(BF16)

    def up_project(sl):
        ab_ref[0, :, sl] = jnp.dot(h_ref[...], wa_ref[:, sl], preferred_element_type=F32)
        ab_ref[1, :, sl] = jnp.dot(h_ref[...], wb_ref[:, sl], preferred_element_type=F32)

    def down_project(n):
        y = jnp.dot(act_ref[0], wo_ref[0:tf, :], preferred_element_type=F32)
        for jj in range(1, nf):
            y += jnp.dot(act_ref[jj], wo_ref[jj * tf:(jj + 1) * tf, :], preferred_element_type=F32)
        o_ref[:, n * tno:(n + 1) * tno] = y

    @pl.when(j == 0)
    def _():
        x = x_ref[...]
        ms = jnp.mean(x * x, axis=-1, keepdims=True)
        y = x * lax.rsqrt(ms + NORM_EPS) * g2_ref[...]
        h_ref[...] = (y * (1.0 + mod_ref[0, 4:5, :]) + mod_ref[0, 3:4, :]).astype(BF16)
        for sl in col_tiles:
            up_project(sl)

    @pl.when((j > 0) & (j < nf))
    def _():
        for sl in col_tiles:
            activate(j - 1, sl)
            up_project(sl)

    @pl.when(j == nf)
    def _():
        for sl in col_tiles:
            activate(nf - 1, sl)
        down_project(0)

    for n in range(1, no):
        @pl.when(j == nf + n)
        def _(n=n):
            down_project(n)

    @pl.when(j == nf + no - 1)
    def _():
        y = o_ref[...]
        ms = jnp.mean(y * y, axis=-1, keepdims=True)
        yn = y * lax.rsqrt(ms + NORM_EPS) * g3_ref[...]
        o_ref[...] = x_ref[...] + mod_ref[0, 5:6, :] * yn


def _ffn(x1, mod, gain2, gain3, w_ffn_in, w_ffn_out, seq):
    t = x1.shape[0]
    tm, tf, tno = 512, 512, 512
    nf = D_FF // tf
    no = D_MODEL // tno
    tiles_per_seq = seq // tm
    return pl.pallas_call(
        functools.partial(_ffn_kernel, nf=nf, no=no),
        grid=(t // tm, nf + no),
        in_specs=[pl.BlockSpec((tm, D_MODEL), lambda m, j: (m, 0)),
                  pl.BlockSpec((1, 6, D_MODEL), lambda m, j: (m // tiles_per_seq, 0, 0)),
                  pl.BlockSpec((1, D_MODEL), lambda m, j: (0, 0)),
                  pl.BlockSpec((1, D_MODEL), lambda m, j: (0, 0)),
                  pl.BlockSpec((D_MODEL, tf), lambda m, j: (0, jnp.minimum(j, nf - 1))),
                  pl.BlockSpec((D_MODEL, tf), lambda m, j: (0, nf + jnp.minimum(j, nf - 1))),
                  pl.BlockSpec((D_FF, tno), lambda m, j: (0, jnp.maximum(j - nf, 0)))],
        out_specs=pl.BlockSpec((tm, D_MODEL), lambda m, j: (m, 0)),
        out_shape=jax.ShapeDtypeStruct((t, D_MODEL), F32),
        scratch_shapes=[pltpu.VMEM((tm, D_MODEL), BF16), pltpu.VMEM((nf, tm, tf), BF16),
                        pltpu.VMEM((2, tm, tf), F32)],
        compiler_params=_params(("parallel", "arbitrary")),
        name="ffn",
    )(x1, mod, gain2, gain3, w_ffn_in, w_ffn_in, w_ffn_out)


def _rope_tables(seq):
    pos = jnp.arange(seq, dtype=F32)
    inv_freq = ROPE_BASE ** (-jnp.arange(RET_QK_DIM // 2, dtype=F32) * (2.0 / RET_QK_DIM))
    ang = pos[:, None] * inv_freq[None, :]
    cos, sin = jnp.cos(ang), jnp.sin(ang)
    return jnp.concatenate([cos, cos], axis=-1), jnp.concatenate([-sin, sin], axis=-1)


def _s5_branch(u_im, params, batch):
    a_re, a_im, log_dt, b_re, b_im, c_re, c_im, d_skip = params
    c, nchunk, _ = u_im.shape
    e_c, f_c, k_c, lam = _s5_compact(*_s5_prep(a_re, a_im, log_dt, b_re, b_im, c_re, c_im, d_skip))
    u_rows = u_im.reshape(c, nchunk * batch, S5_WIDTH)
    x_prev = _s5_state(u_rows, e_c, lam, batch)
    return _s5_out(u_rows, x_prev, f_c, k_c).reshape(c, nchunk, batch * S5_WIDTH)


def kernel(x, c, w_ada, b_ada, norm_gains, w_in, w_ret_out, ssm_a_re, ssm_a_im, ssm_log_dt,
           ssm_b_re, ssm_b_im, ssm_c_re, ssm_c_im, ssm_d, w_s5_glu, w_out, w_ffn_in, w_ffn_out):
    batch, seq, _ = x.shape
    depth = w_in.shape[0]
    cosf, sinf = _rope_tables(seq)
    xt = x.reshape(batch * seq, D_MODEL)
    for l in range(depth):
        gains = norm_gains[l][:, None, :]
        mod = _adaln_mod(c, w_ada[l], b_ada[l])
        ret_in, u_im, gates = _inproj(xt, mod, gains[0], w_in[l].astype(BF16), cosf, sinf, seq)
        retn = _retention(ret_in, batch, seq)
        y_im = _s5_branch(u_im, (ssm_a_re[l], ssm_a_im[l], ssm_log_dt[l], ssm_b_re[l], ssm_b_im[l],
                                 ssm_c_re[l], ssm_c_im[l], ssm_d[l]), batch)
        merged = _merge(retn, y_im, gates, w_ret_out[l].astype(BF16), w_s5_glu[l].astype(BF16), seq)
        x1 = _oproj(merged, w_out[l].astype(BF16), xt, mod, gains[1], seq)
        xt = _ffn(x1, mod, gains[2], gains[3], w_ffn_in[l].astype(BF16), w_ffn_out[l].astype(BF16), seq)
    return xt.reshape(batch, seq, D_MODEL)
```

```python
import functools
import math

import jax
import jax.numpy as jnp
from jax import lax
from jax.experimental import pallas as pl
from jax.experimental.pallas import tpu as pltpu

F32 = jnp.float32
BF16 = jnp.bfloat16

D_MODEL = 2048
RET_HEADS = 8
RET_QK_DIM = 128
RET_V_DIM = 256
RET_QK_WIDTH = RET_HEADS * RET_QK_DIM
RET_V_WIDTH = RET_HEADS * RET_V_DIM
ROPE_BASE = 10000.0
S5_GROUP = 16
S5_WIDTH = D_MODEL // 2
S5_GROUPS = S5_WIDTH // S5_GROUP
S5_STATE = 64
D_FF = 5632
NORM_EPS = 1e-6
GN_EPS = 1e-5

IN_SIZES = (RET_QK_WIDTH, RET_QK_WIDTH, RET_V_WIDTH, RET_V_WIDTH, S5_WIDTH, D_MODEL, D_MODEL)
IN_WIDTH = sum(IN_SIZES)
COL_Q, COL_K, COL_V, COL_GRET, COL_U, COL_GATE_R, COL_GATE_S = (
    sum(IN_SIZES[:i]) for i in range(len(IN_SIZES)))
RETIN_WIDTH = COL_U
GATES_WIDTH = IN_WIDTH - COL_GATE_R

V7X_VMEM_BYTES = 64 * 1024 * 1024
VMEM_LIMIT = V7X_VMEM_BYTES - 8 * 1024 * 1024
MXU_COLS = 256
SUBLANES = 8
LANES = 128

S5_CHUNK = 16
S5_COL_TILE = 128
S5_GROUPS_PER_TILE = S5_COL_TILE // S5_GROUP
S5_GROUP_STATE = 2 * S5_STATE
S5_STATE_TILE = S5_GROUPS_PER_TILE * S5_GROUP_STATE
S5_TILES = S5_WIDTH // S5_COL_TILE
RET_BLOCK = 256


def _params(semantics):
    return pltpu.CompilerParams(dimension_semantics=semantics, vmem_limit_bytes=VMEM_LIMIT)


def _sigmoid(x):
    return 0.5 * jnp.tanh(0.5 * x) + 0.5


def _silu(x):
    half = 0.5 * x
    return half + half * jnp.tanh(half)


def _mod_kernel(c_ref, w_ref, b_ref, o_ref):
    @pl.when(pl.program_id(0) == 0)
    def _():
        o_ref[...] = jnp.broadcast_to(b_ref[...], o_ref.shape)

    c = c_ref[...]
    ca = (c * _sigmoid(c)).astype(BF16)
    o_ref[...] += jnp.dot(ca, w_ref[...].astype(BF16), preferred_element_type=F32)


def _adaln_mod(c, w_ada, b_ada):
    bn = c.shape[0]
    rows = 8
    c_pad = jnp.zeros((rows, D_MODEL), F32).at[:bn].set(c)
    tk = 256
    out = pl.pallas_call(
        _mod_kernel,
        grid=(D_MODEL // tk,),
        in_specs=[pl.BlockSpec((rows, tk), lambda k: (0, k)),
                  pl.BlockSpec((tk, 6 * D_MODEL), lambda k: (k, 0)),
                  pl.BlockSpec((1, 6 * D_MODEL), lambda k: (0, 0))],
        out_specs=pl.BlockSpec((rows, 6 * D_MODEL), lambda k: (0, 0)),
        out_shape=jax.ShapeDtypeStruct((rows, 6 * D_MODEL), F32),
        compiler_params=_params(("arbitrary",)),
        name="mod",
    )(c_pad, w_ada, b_ada.reshape(1, -1))
    return out[:bn].reshape(bn, 6, D_MODEL)


def _inproj_kernel(x_ref, mod_ref, g_ref, w_ref, cos_ref, sin_ref, o_ref, u_ref, gate_ref, h_ref, stage_ref):
    j = pl.program_id(1)

    @pl.when(j == 0)
    def _():
        x = x_ref[...]
        ms = jnp.mean(x * x, axis=-1, keepdims=True)
        y = x * lax.rsqrt(ms + NORM_EPS) * g_ref[...]
        h = y * (1.0 + mod_ref[0, 1:2, :]) + mod_ref[0, 0:1, :]
        h_ref[...] = h.astype(BF16)

    tn = w_ref.shape[1]
    q_tiles = COL_K // tn

    def project(epilogue, out_ref):
        for k in range(tn // MXU_COLS):
            sl = slice(k * MXU_COLS, (k + 1) * MXU_COLS)
            acc = jnp.dot(h_ref[...], w_ref[:, sl], preferred_element_type=F32)
            out_ref[:, sl] = epilogue(acc).astype(BF16)

    @pl.when(j < COL_V // tn)
    def _():
        scale = jnp.where(j >= q_tiles, RET_QK_DIM ** -0.5, 1.0).astype(F32)
        cosf = cos_ref[...] * scale
        sinf = sin_ref[...] * scale

        def rope(acc):
            heads = []
            for hh in range(MXU_COLS // RET_QK_DIM):
                t = acc[:, hh * RET_QK_DIM:(hh + 1) * RET_QK_DIM]
                heads.append(t * cosf + pltpu.roll(t, RET_QK_DIM // 2, 1) * sinf)
            return jnp.concatenate(heads, axis=1)

        project(rope, o_ref)

    @pl.when((j >= COL_V // tn) & (j < COL_GRET // tn))
    def _():
        project(lambda acc: acc, o_ref)

    @pl.when((j >= COL_GRET // tn) & (j < COL_U // tn))
    def _():
        project(_silu, o_ref)

    @pl.when((j >= COL_U // tn) & (j < COL_GATE_R // tn))
    def _():
        chunks = x_ref.shape[0] // S5_CHUNK
        for k in range(tn // MXU_COLS):
            acc = jnp.dot(h_ref[...], w_ref[:, k * MXU_COLS:(k + 1) * MXU_COLS], preferred_element_type=F32)
            for part in range(MXU_COLS // LANES):
                stage_ref[part] = acc[:, part * LANES:(part + 1) * LANES]
            for i in range(S5_CHUNK):
                for part in range(MXU_COLS // LANES):
                    lo = k * MXU_COLS + part * LANES
                    u_ref[i, :, lo:lo + LANES] = (
                        stage_ref[part, pl.ds(i, chunks, stride=S5_CHUNK), :].astype(BF16))

    @pl.when(j >= COL_GATE_R // tn)
    def _():
        project(_sigmoid, gate_ref)


def _inproj(x2, mod, gain, w_in, cosf, sinf, seq):
    t = x2.shape[0]
    tm, tn = 1024, 1024
    assert S5_WIDTH == tn
    ret_tiles = RETIN_WIDTH // tn
    gate_tile0 = COL_GATE_R // tn
    gate_tiles = GATES_WIDTH // tn
    tiles_per_seq = seq // tm
    batch = t // seq
    return pl.pallas_call(
        _inproj_kernel,
        grid=(t // tm, IN_WIDTH // tn),
        in_specs=[pl.BlockSpec((tm, D_MODEL), lambda m, j: (m, 0)),
                  pl.BlockSpec((1, 6, D_MODEL), lambda m, j: (m // tiles_per_seq, 0, 0)),
                  pl.BlockSpec((1, D_MODEL), lambda m, j: (0, 0)),
                  pl.BlockSpec((D_MODEL, tn), lambda m, j: (0, j)),
                  pl.BlockSpec((tm, RET_QK_DIM), lambda m, j: (m % tiles_per_seq, 0)),
                  pl.BlockSpec((tm, RET_QK_DIM), lambda m, j: (m % tiles_per_seq, 0))],
        out_specs=[pl.BlockSpec((tm, tn), lambda m, j: (m, jnp.minimum(j, ret_tiles - 1))),
                   pl.BlockSpec((S5_CHUNK, None, tm // S5_CHUNK, tn),
                                lambda m, j: (0, m // tiles_per_seq, m % tiles_per_seq, 0)),
                   pl.BlockSpec((tm, tn), lambda m, j: (m, jnp.clip(j - gate_tile0, 0, gate_tiles - 1)))],
        out_shape=[jax.ShapeDtypeStruct((t, RETIN_WIDTH), BF16),
                   jax.ShapeDtypeStruct((S5_CHUNK, batch, seq // S5_CHUNK, S5_WIDTH), BF16),
                   jax.ShapeDtypeStruct((t, GATES_WIDTH), BF16)],
        scratch_shapes=[pltpu.VMEM((tm, D_MODEL), BF16), pltpu.VMEM((MXU_COLS // LANES, tm, LANES), F32)],
        compiler_params=_params(("parallel", "arbitrary")),
        name="inproj",
    )(x2, mod, gain, w_in, cosf, sinf)


def _ret_kernel(x_ref, o_ref, state_ref, decay_ref, wcross_ref, wstate_ref):
    n = pl.program_id(1)
    cr = x_ref.shape[0]

    @pl.when(n == 0)
    def _():
        state_ref[...] = jnp.zeros_like(state_ref)

    @pl.when((pl.program_id(0) == 0) & (n == 0))
    def _():
        row = lax.broadcasted_iota(jnp.int32, (cr, cr), 0)
        col = lax.broadcasted_iota(jnp.int32, (cr, cr), 1)
        rel = (row - col).astype(F32)
        idx_v = lax.broadcasted_iota(jnp.int32, (cr, RET_V_DIM), 0).astype(F32)
        idx_k = lax.broadcasted_iota(jnp.int32, (cr, RET_QK_DIM), 0).astype(F32)
        for h in range(RET_HEADS):
            log_g = math.log1p(-(2.0 ** (-5.0 - h)))
            decay_ref[h] = jnp.where(rel >= 0.0, jnp.exp(jnp.maximum(rel, 0.0) * log_g), 0.0)
            wcross_ref[h] = jnp.exp((idx_v + 1.0) * log_g)
            wstate_ref[h] = jnp.exp((cr - 1.0 - idx_k) * log_g)

    for h in range(RET_HEADS):
        chunk_decay = math.exp(cr * math.log1p(-(2.0 ** (-5.0 - h))))
        vs = slice(h * RET_V_DIM, (h + 1) * RET_V_DIM)
        q = x_ref[:, COL_Q + h * RET_QK_DIM:COL_Q + (h + 1) * RET_QK_DIM]
        k = x_ref[:, COL_K + h * RET_QK_DIM:COL_K + (h + 1) * RET_QK_DIM]
        v = x_ref[:, COL_V + h * RET_V_DIM:COL_V + (h + 1) * RET_V_DIM]
        gate = x_ref[:, COL_GRET + h * RET_V_DIM:COL_GRET + (h + 1) * RET_V_DIM]
        scores = lax.dot_general(q, k, (((1,), (1,)), ((), ())), preferred_element_type=F32)
        inner = jnp.dot((scores * decay_ref[h]).astype(BF16), v, preferred_element_type=F32)
        state = state_ref[h]
        cross = jnp.dot(q, state.astype(BF16), preferred_element_type=F32) * wcross_ref[h]
        ret = inner + cross
        mu = jnp.mean(ret, axis=-1, keepdims=True)
        dev = ret - mu
        var = jnp.mean(dev * dev, axis=-1, keepdims=True)
        normed = dev * lax.rsqrt(var + GN_EPS)
        o_ref[:, vs] = (normed * gate.astype(F32)).astype(BF16)

        kw = (k.astype(F32) * wstate_ref[h]).astype(BF16)
        kv = lax.dot_general(kw, v, (((0,), (0,)), ((), ())), preferred_element_type=F32)
        state_ref[h] = chunk_decay * state + kv


def _retention(proj, batch, seq):
    t = proj.shape[0]
    cr = RET_BLOCK
    nblk = seq // cr
    return pl.pallas_call(
        _ret_kernel,
        grid=(batch, nblk),
        in_specs=[pl.BlockSpec((cr, RETIN_WIDTH), lambda b, n: (b * nblk + n, 0))],
        out_specs=pl.BlockSpec((cr, RET_V_WIDTH), lambda b, n: (b * nblk + n, 0)),
        out_shape=jax.ShapeDtypeStruct((t, RET_V_WIDTH), BF16),
        scratch_shapes=[pltpu.VMEM((RET_HEADS, RET_QK_DIM, RET_V_DIM), F32),
                        pltpu.VMEM((RET_HEADS, cr, cr), F32),
                        pltpu.VMEM((RET_HEADS, cr, RET_V_DIM), F32),
                        pltpu.VMEM((RET_HEADS, cr, RET_QK_DIM), F32)],
        compiler_params=_params(("arbitrary", "arbitrary")),
        name="ret",
    )(proj)


def _cmul(ar, ai, br, bi):
    return ar * br - ai * bi, ar * bi + ai * br


def _s5prep_kernel(are_ref, aim_ref, ldt_ref, bre_ref, bim_ref, cre_ref, cim_ref, d_ref,
                   er_ref, ei_ref, fr_ref, fi_ref, k_ref, lr_ref, li_ref):
    ar = are_ref[...]
    ai = aim_ref[...]
    dt = jnp.exp(ldt_ref[...])
    mag = jnp.exp(ar * dt)
    lbr = mag * jnp.cos(ai * dt)
    lbi = mag * jnp.sin(ai * dt)
    nr = lbr - 1.0
    den = ar * ar + ai * ai
    f_re = (nr * ar + lbi * ai) / den
    f_im = (lbi * ar - nr * ai) / den
    bbr, bbi = _cmul(f_re, f_im, bre_ref[...], bim_ref[...])

    rows = S5_CHUNK * S5_GROUP
    gb, _, p = ar.shape
    lag = lax.broadcasted_iota(jnp.int32, (gb, S5_CHUNK, p), 1).astype(F32)

    def lam_pow(k):
        m = jnp.exp(ar * dt * k)
        return m * jnp.cos(ai * dt * k), m * jnp.sin(ai * dt * k)

    def tile_rows(a):
        return jnp.concatenate([a] * S5_CHUNK, axis=1)

    def repeat_rows(a):
        return jnp.concatenate(
            [jnp.broadcast_to(a[:, k:k + 1, :], (gb, S5_GROUP, p)) for k in range(S5_CHUNK)], axis=1)

    p0r, p0i = lam_pow(lag)
    p1r, p1i = _cmul(p0r, p0i, lbr, lbi)
    er, ei = _cmul(repeat_rows(p0r), repeat_rows(p0i), tile_rows(bbr), tile_rows(bbi))
    er_ref[...] = er
    ei_ref[...] = ei
    fr, fi = _cmul(repeat_rows(p1r), repeat_rows(p1i),
                   tile_rows(cre_ref[...]), tile_rows(cim_ref[...]))
    fr_ref[...] = fr
    fi_ref[...] = -fi

    dims = (((2,), (2,)), ((0,), (0,)))
    kern = (lax.dot_general(cre_ref[...], er, dims, preferred_element_type=F32,
                            precision=lax.Precision.HIGHEST)
            - lax.dot_general(cim_ref[...], ei, dims, preferred_element_type=F32,
                              precision=lax.Precision.HIGHEST))
    hrow = lax.broadcasted_iota(jnp.int32, (1, S5_GROUP, rows), 1)
    ncol = lax.broadcasted_iota(jnp.int32, (1, S5_GROUP, rows), 2)
    k_ref[...] = kern + jnp.where(hrow == ncol, d_ref[...], 0.0)

    lcr, lci = lam_pow(float(S5_CHUNK))
    lr_ref[...] = lcr
    li_ref[...] = lci


def _s5_prep(a_re, a_im, log_dt, b_re, b_im, c_re, c_im, d_skip):
    g, p, hg = S5_GROUPS, S5_STATE, S5_GROUP
    gb = 8
    rows = S5_CHUNK * hg
    gp = lambda: pl.BlockSpec((gb, 1, p), lambda i: (i, 0, 0))
    ghp = lambda: pl.BlockSpec((gb, hg, p), lambda i: (i, 0, 0))
    big = lambda: pl.BlockSpec((gb, rows, p), lambda i: (i, 0, 0))
    outs = pl.pallas_call(
        _s5prep_kernel,
        grid=(g // gb,),
        in_specs=[gp(), gp(), gp(), ghp(), ghp(), ghp(), ghp(),
                  pl.BlockSpec((gb, hg, 1), lambda i: (i, 0, 0))],
        out_specs=[big(), big(), big(), big(),
                   pl.BlockSpec((gb, hg, rows), lambda i: (i, 0, 0)), gp(), gp()],
        out_shape=[jax.ShapeDtypeStruct((g, rows, p), F32)] * 4
        + [jax.ShapeDtypeStruct((g, hg, rows), F32)]
        + [jax.ShapeDtypeStruct((g, 1, p), F32)] * 2,
        compiler_params=_params(("arbitrary",)),
        name="s5prep",
    )(a_re[:, None, :], a_im[:, None, :],
      jnp.broadcast_to(log_dt[:, None, None], (g, 1, p)),
      jnp.swapaxes(b_re, 1, 2), jnp.swapaxes(b_im, 1, 2), c_re, c_im, d_skip[:, :, None])
    return outs


def _s5_compact(er, ei, fr, fin, kern, lcr, lci):
    g, hg, p, c = S5_GROUPS, S5_GROUP, S5_STATE, S5_CHUNK
    e = jnp.concatenate([er, ei], axis=-1).reshape(g, c, hg, 2 * p)
    e_c = jnp.flip(e.transpose(1, 0, 2, 3), axis=0).reshape(c, g * hg, 2 * p).astype(BF16)
    f = jnp.concatenate([fr, fin], axis=-1).reshape(g, c, hg, 2 * p)
    f_c = f.transpose(1, 3, 0, 2).reshape(c, 2 * p, g * hg).astype(BF16)
    k_c = kern.reshape(g, hg, c, hg).transpose(2, 3, 0, 1).reshape(c, hg, g * hg).astype(BF16)
    lam_shape = (S5_TILES, S5_GROUPS_PER_TILE, 1, S5_GROUP_STATE)
    lam = jnp.stack([jnp.concatenate([lcr, lcr], axis=-1).reshape(lam_shape),
                     jnp.concatenate([lci, lci], axis=-1).reshape(lam_shape)], axis=1)
    return e_c, f_c, k_c, lam


def _s5state_kernel(u_ref, e_ref, lam_ref, o_ref, w_ref, acc_ref, swp_ref, *, batch, nchunk):
    c = u_ref.shape[0]
    ct, gs, hg = S5_COL_TILE, S5_GROUP_STATE, S5_GROUP

    @pl.when(pl.program_id(0) == 0)
    def _():
        w_ref[...] = jnp.zeros_like(w_ref)

    for i in range(c):
        for gl in range(S5_GROUPS_PER_TILE):
            w_ref[i * ct + gl * hg:i * ct + (gl + 1) * hg, gl * gs:(gl + 1) * gs] = (
                e_ref[i, gl * hg:(gl + 1) * hg, :])

    lhs = jnp.concatenate([u_ref[i] for i in range(c)], axis=1)
    s = jnp.dot(lhs, w_ref[...], preferred_element_type=F32)
    width = s.shape[1]
    is_re = (lax.broadcasted_iota(jnp.int32, (1, width), 1) % gs) < S5_STATE
    sw = jnp.where(is_re, pltpu.roll(s, width - S5_STATE, 1), pltpu.roll(s, S5_STATE, 1))
    assert gs == LANES
    groups = width // LANES
    for g in range(groups):
        for b in range(batch):
            rows = slice(b * nchunk, (b + 1) * nchunk)
            acc_ref[g, pl.ds(b, nchunk, stride=batch), :] = s[rows, g * LANES:(g + 1) * LANES]
            swp_ref[g, pl.ds(b, nchunk, stride=batch), :] = sw[rows, g * LANES:(g + 1) * LANES]

    lr = lam_ref[0, 0]
    li = lam_ref[0, 1]
    lx = jnp.where(lax.broadcasted_iota(jnp.int32, li.shape, 2) < S5_STATE, -li, li)

    sub = SUBLANES // batch
    first = lax.broadcasted_iota(jnp.int32, (groups, SUBLANES, LANES), 1) < batch

    def step(m, carry):
        c, cs = carry
        r0 = pl.multiple_of(m * SUBLANES, SUBLANES)
        v = acc_ref[:, pl.ds(r0, SUBLANES), :]
        vs = swp_ref[:, pl.ds(r0, SUBLANES), :]
        t = lr * c + lx * cs + v
        ts = lr * cs - lx * c + vs
        t_sh = pltpu.roll(t, batch, 1)
        ts_sh = pltpu.roll(ts, batch, 1)
        t2 = lr * t_sh + lx * ts_sh + v
        t2s = lr * ts_sh - lx * t_sh + vs
        acc_ref[:, pl.ds(r0, SUBLANES), :] = jnp.where(first, c, t_sh)
        return (jnp.where(first, pltpu.roll(t2, batch, 1), t2),
                jnp.where(first, pltpu.roll(t2s, batch, 1), t2s))

    assert sub == 2 and (batch * nchunk) % SUBLANES == 0
    zero = jnp.zeros((groups, SUBLANES, LANES), F32)
    lax.fori_loop(0, batch * nchunk // SUBLANES, step, (zero, zero))
    for g in range(groups):
        for b in range(batch):
            o_ref[b * nchunk:(b + 1) * nchunk, g * LANES:(g + 1) * LANES] = (
                acc_ref[g, pl.ds(b, nchunk, stride=batch), :].astype(BF16))


def _s5_state(u_im, e_c, lam, batch):
    c, rows, _ = u_im.shape
    return pl.pallas_call(
        functools.partial(_s5state_kernel, batch=batch, nchunk=rows // batch),
        grid=(S5_TILES,),
        in_specs=[pl.BlockSpec((c, rows, S5_COL_TILE), lambda t: (0, 0, t)),
                  pl.BlockSpec((c, S5_COL_TILE, S5_GROUP_STATE), lambda t: (0, t, 0)),
                  pl.BlockSpec((1, 2, S5_GROUPS_PER_TILE, 1, LANES), lambda t: (t, 0, 0, 0, 0))],
        out_specs=pl.BlockSpec((rows, S5_STATE_TILE), lambda t: (0, t)),
        out_shape=jax.ShapeDtypeStruct((rows, S5_TILES * S5_STATE_TILE), BF16),
        scratch_shapes=[pltpu.VMEM((c * S5_COL_TILE, S5_STATE_TILE), BF16),
                        pltpu.VMEM((S5_GROUPS_PER_TILE, rows, LANES), F32),
                        pltpu.VMEM((S5_GROUPS_PER_TILE, rows, LANES), F32)],
        compiler_params=_params(("arbitrary",)),
        name="s5state",
    )(u_im, e_c, lam)


def _s5out_kernel(u_ref, x_ref, f_ref, k_ref, o_ref, w_ref, lhs_ref):
    c = u_ref.shape[0]
    ct, gs, hg, st = S5_COL_TILE, S5_GROUP_STATE, S5_GROUP, S5_STATE_TILE

    @pl.when(pl.program_id(0) == 0)
    def _():
        w_ref[...] = jnp.zeros_like(w_ref)

    lane_group = lax.broadcasted_iota(jnp.int32, (gs, ct), 1) // hg
    for i in range(c):
        f_i = f_ref[i]
        for gl in range(S5_GROUPS_PER_TILE):
            w_ref[gl * gs:(gl + 1) * gs, i * ct:(i + 1) * ct] = jnp.where(
                lane_group == gl, f_i, jnp.zeros_like(f_i))
    lane_group = lax.broadcasted_iota(jnp.int32, (hg, ct), 1) // hg
    for j in range(c):
        k_j = k_ref[j]
        lag_block = jnp.concatenate(
            [jnp.where(lane_group == gl, k_j, jnp.zeros_like(k_j)) for gl in range(S5_GROUPS_PER_TILE)],
            axis=0)
        for i0 in range(c - j):
            w_ref[st + i0 * ct:st + (i0 + 1) * ct, (i0 + j) * ct:(i0 + j + 1) * ct] = lag_block

    lhs_ref[:, 0:st] = x_ref[...]
    for i in range(c):
        lhs_ref[:, st + i * ct:st + (i + 1) * ct] = u_ref[i]

    for n in range(c // 2):
        k = st + 2 * ct * (n + 1)
        acc = jnp.dot(lhs_ref[:, 0:k], w_ref[0:k, 2 * n * ct:2 * (n + 1) * ct], preferred_element_type=F32)
        y = jax.nn.gelu(acc).astype(BF16)
        o_ref[2 * n] = y[:, 0:ct]
        o_ref[2 * n + 1] = y[:, ct:2 * ct]


def _s5_out(u_im, x_prev, f_c, k_c):
    c, rows, width = u_im.shape
    return pl.pallas_call(
        _s5out_kernel,
        grid=(S5_TILES,),
        in_specs=[pl.BlockSpec((c, rows, S5_COL_TILE), lambda t: (0, 0, t)),
                  pl.BlockSpec((rows, S5_STATE_TILE), lambda t: (0, t)),
                  pl.BlockSpec((c, S5_GROUP_STATE, S5_COL_TILE), lambda t: (0, 0, t)),
                  pl.BlockSpec((c, S5_GROUP, S5_COL_TILE), lambda t: (0, 0, t))],
        out_specs=pl.BlockSpec((c, rows, S5_COL_TILE), lambda t: (0, 0, t)),
        out_shape=jax.ShapeDtypeStruct((c, rows, width), BF16),
        scratch_shapes=[pltpu.VMEM((S5_STATE_TILE + c * S5_COL_TILE, c * S5_COL_TILE), BF16),
                        pltpu.VMEM((rows, S5_STATE_TILE + c * S5_COL_TILE), BF16)],
        compiler_params=_params(("arbitrary",)),
        name="s5out",
    )(u_im, x_prev, f_c, k_c)


def _merge_kernel(ret_ref, ssm_ref, wro_ref, wga_ref, wgb_ref, gr_ref, gs_ref, o_ref, stage_ref):
    c, chunks, width = ssm_ref.shape
    ssm = ssm_ref[...].reshape(c * chunks, width)
    for k in range(o_ref.shape[1] // MXU_COLS):
        sl = slice(k * MXU_COLS, (k + 1) * MXU_COLS)
        y_ret = jnp.dot(ret_ref[...], wro_ref[:, sl], preferred_element_type=F32)
        glu_a = jnp.dot(ssm, wga_ref[:, sl], preferred_element_type=F32)
        glu_b = jnp.dot(ssm, wgb_ref[:, sl], preferred_element_type=F32)
        y_s5 = glu_a * _sigmoid(glu_b)
        for part in range(MXU_COLS // LANES):
            for i in range(c):
                stage_ref[part, pl.ds(i, chunks, stride=c), :] = (
                    y_s5[i * chunks:(i + 1) * chunks, part * LANES:(part + 1) * LANES])
        y_s5 = jnp.concatenate([stage_ref[part] for part in range(MXU_COLS // LANES)], axis=1)
        merged = gr_ref[:, sl].astype(F32) * y_ret + gs_ref[:, sl].astype(F32) * y_s5
        o_ref[:, sl] = merged.astype(BF16)


def _merge(retn, y_im, gates, w_ret_out, w_s5_glu, seq):
    t = retn.shape[0]
    tm, tn = 1024, 1024
    nb = D_MODEL // tn
    tiles_per_seq = seq // tm
    return pl.pallas_call(
        _merge_kernel,
        grid=(t // tm, nb),
        in_specs=[pl.BlockSpec((tm, RET_V_WIDTH), lambda m, j: (m, 0)),
                  pl.BlockSpec((S5_CHUNK, None, tm // S5_CHUNK, S5_WIDTH),
                               lambda m, j: (0, m // tiles_per_seq, m % tiles_per_seq, 0)),
                  pl.BlockSpec((RET_V_WIDTH, tn), lambda m, j: (0, j)),
                  pl.BlockSpec((S5_WIDTH, tn), lambda m, j: (0, j)),
                  pl.BlockSpec((S5_WIDTH, tn), lambda m, j: (0, nb + j)),
                  pl.BlockSpec((tm, tn), lambda m, j: (m, j)),
                  pl.BlockSpec((tm, tn), lambda m, j: (m, nb + j))],
        out_specs=pl.BlockSpec((tm, tn), lambda m, j: (m, j)),
        out_shape=jax.ShapeDtypeStruct((t, D_MODEL), BF16),
        scratch_shapes=[pltpu.VMEM((MXU_COLS // LANES, tm, LANES), F32)],
        compiler_params=_params(("parallel", "arbitrary")),
        name="merge",
    )(retn, y_im, w_ret_out, w_s5_glu, w_s5_glu, gates, gates)


def _oproj_kernel(a_ref, w_ref, x_ref, mod_ref, g_ref, o_ref):
    y = jnp.dot(a_ref[...], w_ref[...], preferred_element_type=F32)
    ms = jnp.mean(y * y, axis=-1, keepdims=True)
    yn = y * lax.rsqrt(ms + NORM_EPS) * g_ref[...]
    o_ref[...] = x_ref[...] + mod_ref[0, 2:3, :] * yn


def _oproj(merged, w_out, x2, mod, gain, seq):
    t = merged.shape[0]
    tm = 512
    tiles_per_seq = seq // tm
    return pl.pallas_call(
        _oproj_kernel,
        grid=(t // tm,),
        in_specs=[pl.BlockSpec((tm, D_MODEL), lambda m: (m, 0)),
                  pl.BlockSpec((D_MODEL, D_MODEL), lambda m: (0, 0)),
                  pl.BlockSpec((tm, D_MODEL), lambda m: (m, 0)),
                  pl.BlockSpec((1, 6, D_MODEL), lambda m: (m // tiles_per_seq, 0, 0)),
                  pl.BlockSpec((1, D_MODEL), lambda m: (0, 0))],
        out_specs=pl.BlockSpec((tm, D_MODEL), lambda m: (m, 0)),
        out_shape=jax.ShapeDtypeStruct((t, D_MODEL), F32),
        compiler_params=_params(("parallel",)),
        name="oproj",
    )(merged, w_out, x2, mod, gain)


def _ffn_kernel(x_ref, mod_ref, g2_ref, g3_ref, wa_ref, wb_ref, wo_ref, o_ref, h_ref, act_ref, ab_ref,
                *, nf, no):
    j = pl.program_id(1)
    tf = wa_ref.shape[1]
    tno = wo_ref.shape[1]
    col_tiles = [slice(k * MXU_COLS, (k + 1) * MXU_COLS) for k in range(tf // MXU_COLS)]

    def activate(tile, sl):
        act_ref[tile, :, sl] = (_silu(ab_ref[0, :, sl]) * ab_ref[1, :, sl]).astype(BF16)

    def up_project(sl):
        ab_ref[0, :, sl] = jnp.dot(h_ref[...], wa_ref[:, sl], preferred_element_type=F32)
        ab_ref[1, :, sl] = jnp.dot(h_ref[...], wb_ref[:, sl], preferred_element_type=F32)

    def down_project(n):
        y = jnp.dot(act_ref[0], wo_ref[0:tf, :], preferred_element_type=F32)
        for jj in range(1, nf):
            y += jnp.dot(act_ref[jj], wo_ref[jj * tf:(jj + 1) * tf, :], preferred_element_type=F32)
        o_ref[:, n * tno:(n + 1) * tno] = y

    @pl.when(j == 0)
    def _():
        x = x_ref[...]
        ms = jnp.mean(x * x, axis=-1, keepdims=True)
        y = x * lax.rsqrt(ms + NORM_EPS) * g2_ref[...]
        h_ref[...] = (y * (1.0 + mod_ref[0, 4:5, :]) + mod_ref[0, 3:4, :]).astype(BF16)
        for sl in col_tiles:
            up_project(sl)

    @pl.when((j > 0) & (j < nf))
    def _():
        for sl in col_tiles:
            activate(j - 1, sl)
            up_project(sl)

    @pl.when(j == nf)
    def _():
        for sl in col_tiles:
            activate(nf - 1, sl)
        down_project(0)

    for n in range(1, no):
        @pl.when(j == nf + n)
        def _(n=n):
            down_project(n)

    @pl.when(j == nf + no - 1)
    def _():
        y = o_ref[...]
        ms = jnp.mean(y * y, axis=-1, keepdims=True)
        yn = y * lax.rsqrt(ms + NORM_EPS) * g3_ref[...]
        o_ref[...] = x_ref[...] + mod_ref[0, 5:6, :] * yn


def _ffn(x1, mod, gain2, gain3, w_ffn_in, w_ffn_out, seq):
    t = x1.shape[0]
    tm, tf, tno = 1024, 512, 256
    nf = D_FF // tf
    no = D_MODEL // tno
    tiles_per_seq = seq // tm
    single = pl.Buffered(1)
    return pl.pallas_call(
        functools.partial(_ffn_kernel, nf=nf, no=no),
        grid=(t // tm, nf + no),
        in_specs=[pl.BlockSpec((tm, D_MODEL), lambda m, j: (m, 0), pipeline_mode=single),
                  pl.BlockSpec((1, 6, D_MODEL), lambda m, j: (m // tiles_per_seq, 0, 0)),
                  pl.BlockSpec((1, D_MODEL), lambda m, j: (0, 0)),
                  pl.BlockSpec((1, D_MODEL), lambda m, j: (0, 0)),
                  pl.BlockSpec((D_MODEL, tf), lambda m, j: (0, jnp.minimum(j, nf - 1))),
                  pl.BlockSpec((D_MODEL, tf), lambda m, j: (0, nf + jnp.minimum(j, nf - 1))),
                  pl.BlockSpec((D_FF, tno), lambda m, j: (0, jnp.maximum(j - nf, 0)))],
        out_specs=pl.BlockSpec((tm, D_MODEL), lambda m, j: (m, 0), pipeline_mode=single),
        out_shape=jax.ShapeDtypeStruct((t, D_MODEL), F32),
        scratch_shapes=[pltpu.VMEM((tm, D_MODEL), BF16), pltpu.VMEM((nf, tm, tf), BF16),
                        pltpu.VMEM((2, tm, tf), F32)],
        compiler_params=_params(("parallel", "arbitrary")),
        name="ffn",
    )(x1, mod, gain2, gain3, w_ffn_in, w_ffn_in, w_ffn_out)


def _rope_tables(seq):
    pos = jnp.arange(seq, dtype=F32)
    inv_freq = ROPE_BASE ** (-jnp.arange(RET_QK_DIM // 2, dtype=F32) * (2.0 / RET_QK_DIM))
    ang = pos[:, None] * inv_freq[None, :]
    cos, sin = jnp.cos(ang), jnp.sin(ang)
    return jnp.concatenate([cos, cos], axis=-1), jnp.concatenate([-sin, sin], axis=-1)


def _s5_branch(u_im, params):
    a_re, a_im, log_dt, b_re, b_im, c_re, c_im, d_skip = params
    c, batch, nchunk, width = u_im.shape
    e_c, f_c, k_c, lam = _s5_compact(*_s5_prep(a_re, a_im, log_dt, b_re, b_im, c_re, c_im, d_skip))
    u_rows = u_im.reshape(c, batch * nchunk, width)
    x_prev = _s5_state(u_rows, e_c, lam, batch)
    return _s5_out(u_rows, x_prev, f_c, k_c).reshape(c, batch, nchunk, width)


def kernel(x, c, w_ada, b_ada, norm_gains, w_in, w_ret_out, ssm_a_re, ssm_a_im, ssm_log_dt,
           ssm_b_re, ssm_b_im, ssm_c_re, ssm_c_im, ssm_d, w_s5_glu, w_out, w_ffn_in, w_ffn_out):
    batch, seq, _ = x.shape
    depth = w_in.shape[0]
    cosf, sinf = _rope_tables(seq)
    xt = x.reshape(batch * seq, D_MODEL)
    for l in range(depth):
        gains = norm_gains[l][:, None, :]
        mod = _adaln_mod(c, w_ada[l], b_ada[l])
        ret_in, u_im, gates = _inproj(xt, mod, gains[0], w_in[l].astype(BF16), cosf, sinf, seq)
        retn = _retention(ret_in, batch, seq)
        y_im = _s5_branch(u_im, (ssm_a_re[l], ssm_a_im[l], ssm_log_dt[l], ssm_b_re[l], ssm_b_im[l],
                                 ssm_c_re[l], ssm_c_im[l], ssm_d[l]))
        merged = _merge(retn, y_im, gates, w_ret_out[l].astype(BF16), w_s5_glu[l].astype(BF16), seq)
        x1 = _oproj(merged, w_out[l].astype(BF16), xt, mod, gains[1], seq)
        xt = _ffn(x1, mod, gains[2], gains[3], w_ffn_in[l].astype(BF16), w_ffn_out[l].astype(BF16), seq)
    return xt.reshape(batch, seq, D_MODEL)
```
